```python
import math
import jax
import jax.numpy as jnp
from jax import lax
import numpy as np

D_MODEL = 1024
BATCH = 8
SEQ = 2048
DEPTH = 2

RW_HEADS = 6
RW_HEAD_DIM = 64
RW_WIDTH = RW_HEADS * RW_HEAD_DIM
SB_HEADS = 6
SB_HEAD_DIM = 64
SB_WIDTH = SB_HEADS * SB_HEAD_DIM
POOL_WINDOWS = (2, 4, 8, 16)
POOL_GROUPS = 4
POOL_GROUP_DIM = 64
POOL_WIDTH = POOL_GROUPS * POOL_GROUP_DIM
MIX_WIDTH = RW_WIDTH + SB_WIDTH + POOL_WIDTH

RW_DECAY_RANK = 32
RW_ICLR_RANK = 32
RW_VRES_RANK = 32
RW_GATE_RANK = 64
RW_GN_EPS = 64e-5

RW_COLS = 3 * RW_WIDTH + RW_DECAY_RANK + RW_ICLR_RANK + RW_GATE_RANK
SB_COLS = 3 * SB_WIDTH
IN_COLS = RW_COLS + SB_COLS + POOL_WIDTH

SB_BLOCK = 128

N_EXPERTS = 64
TOP_K = 6
N_GROUPS = 8
TOPK_GROUPS = 4
EXPERT_DIM = 256
ROUTED_SCALE = 2.5
MOE_BLOCK = 128

RMS_EPS = 1e-6

kernel_name = "hybrid_rwkv7_stickbreak_pool_moe_adaln"


def rms_norm(x, eps=RMS_EPS):
    xf = x.astype(jnp.float32)
    return (xf * lax.rsqrt(jnp.mean(xf * xf, axis=-1, keepdims=True) + eps)).astype(x.dtype)


def token_shift(u, mu):
    prev = jnp.pad(u, ((0, 0), (1, 0), (0, 0)))[:, :-1]
    return u + (prev - u) * mu


def split_heads(t, n_heads, head_dim):
    return t.reshape(t.shape[0], t.shape[1], n_heads, head_dim)


def rwkv7_scan(r, w, k, v, a, b):
    B, S, H, N = r.shape
    xs = (jnp.moveaxis(r, 1, 0), jnp.moveaxis(w, 1, 0), jnp.moveaxis(k, 1, 0),
          jnp.moveaxis(v, 1, 0), jnp.moveaxis(a, 1, 0), jnp.moveaxis(b, 1, 0))

    def step(state, inp):
        r_t, w_t, k_t, v_t, a_t, b_t = inp
        sa = jnp.einsum('bhvk,bhk->bhv', state, a_t)
        state = (state * w_t[:, :, None, :] + sa[..., None] * b_t[:, :, None, :]
                 + v_t[..., None] * k_t[:, :, None, :])
        return state, jnp.einsum('bhvk,bhk->bhv', state, r_t)

    state0 = jnp.zeros((B, H, N, N), jnp.float32)
    _, ys = lax.scan(step, state0, xs)
    return jnp.moveaxis(ys, 0, 1)


def rwkv7_time_mix(p_rw, v_first, v_gate, w0, w2, a0, a2, g2, k_k, k_a, r_k, ln_g, ln_b):
    B, S, _ = p_rw.shape
    C = RW_WIDTH
    f = p_rw.astype(jnp.float32)
    r = f[..., :C]
    k = f[..., C:2 * C]
    v = f[..., 2 * C:3 * C]
    o = 3 * C
    xw = f[..., o:o + RW_DECAY_RANK]
    o += RW_DECAY_RANK
    xa = f[..., o:o + RW_ICLR_RANK]
    o += RW_ICLR_RANK
    xg = f[..., o:o + RW_GATE_RANK]

    w_log = -jax.nn.softplus(-(w0 + jnp.tanh(xw) @ w2)) - 0.5
    decay = jnp.exp(-jnp.exp(w_log))
    a = jax.nn.sigmoid(a0 + xa @ a2)
    if v_gate is not None:
        v = v + (v_first - v) * v_gate
    g = jax.nn.sigmoid(xg) @ g2

    kk = split_heads(k * k_k, RW_HEADS, RW_HEAD_DIM)
    kk = kk / jnp.maximum(jnp.sqrt(jnp.sum(kk * kk, axis=-1, keepdims=True)), 1e-12)
    k = k * (1.0 + (a - 1.0) * k_a)

    rh = split_heads(r, RW_HEADS, RW_HEAD_DIM)
    kh = split_heads(k, RW_HEADS, RW_HEAD_DIM)
    vh = split_heads(v, RW_HEADS, RW_HEAD_DIM)
    ah = split_heads(a, RW_HEADS, RW_HEAD_DIM)
    wh = split_heads(decay, RW_HEADS, RW_HEAD_DIM)
    y = rwkv7_scan(rh, wh, kh, vh, -kk, kk * ah)

    mean = jnp.mean(y, axis=-1, keepdims=True)
    var = jnp.mean(jnp.square(y - mean), axis=-1, keepdims=True)
    y = ((y - mean) * lax.rsqrt(var + RW_GN_EPS)).reshape(B, S, C) * ln_g + ln_b
    bonus = (jnp.sum(rh * kh * r_k, axis=-1, keepdims=True) * vh).reshape(B, S, C)
    return (y + bonus) * g, v


def stick_breaking_attention(q, k, v):
    S = q.shape[1]
    scale = SB_HEAD_DIM ** -0.5
    outs = []
    for i in range(S // SB_BLOCK):
        q0 = i * SB_BLOCK
        q1 = q0 + SB_BLOCK
        z = jnp.einsum('bqhd,bkhd->bhqk', q[:, q0:q1], k[:, :q1],
                       preferred_element_type=jnp.float32) * scale
        valid = jnp.arange(q0, q1)[:, None] > jnp.arange(q1)[None, :]
        log_not = jnp.where(valid, jax.nn.log_sigmoid(-z), 0.0)
        suffix = lax.cumsum(log_not, axis=3, reverse=True) - log_not
        att = jnp.where(valid, jnp.exp(jax.nn.log_sigmoid(z) + suffix), 0.0)
        outs.append(jnp.einsum('bhqk,bkhd->bqhd', att.astype(v.dtype), v[:, :q1]))
    return jnp.concatenate(outs, axis=1)


def multiscale_pool(u, w_pool, pool_scale):
    B, S, _ = u.shape
    ug = u.astype(jnp.float32).reshape(B, S, POOL_GROUPS, POOL_GROUP_DIM)
    positions = jnp.arange(1, S + 1, dtype=jnp.float32)
    pooled = []
    for g, win in enumerate(POOL_WINDOWS):
        xg = ug[:, :, g]
        cs = jnp.cumsum(xg, axis=1)
        cs_lag = jnp.pad(cs, ((0, 0), (win, 0), (0, 0)))[:, :S]
        count = jnp.minimum(positions, float(win))[None, :, None]
        pooled.append((cs - cs_lag) / count - xg)
    pooled = jnp.stack(pooled, axis=2)
    mixed = jnp.einsum('bsgc,gcd->bsgd', pooled, w_pool.astype(jnp.float32))
    return (mixed.reshape(B, S, POOL_WIDTH) * pool_scale).astype(u.dtype)


def swiglu(h, w_gate, w_up, w_down):
    return (jax.nn.silu(h @ w_gate) * (h @ w_up)) @ w_down


def routed_experts(hf, top_idx, top_w, w_gate, w_up, w_down):
    T, D = hf.shape
    n_assign = T * TOP_K
    n_blocks = -(-(n_assign + N_EXPERTS * MOE_BLOCK) // MOE_BLOCK)
    cap = n_blocks * MOE_BLOCK
    flat_e = top_idx.reshape(-1)
    flat_tok = jnp.repeat(jnp.arange(T, dtype=jnp.int32), TOP_K)
    flat_w = top_w.reshape(-1)
    order = jnp.argsort(flat_e)
    e_sorted = flat_e[order]
    counts = jnp.bincount(flat_e, length=N_EXPERTS)
    padded = (counts + MOE_BLOCK - 1) // MOE_BLOCK * MOE_BLOCK
    start = jnp.cumsum(counts) - counts
    pend = jnp.cumsum(padded)
    pstart = pend - padded
    dest = pstart[e_sorted] + (jnp.arange(n_assign) - start[e_sorted])
    slot_tok = jnp.full((cap,), T, jnp.int32).at[dest].set(flat_tok[order])
    slot_w = jnp.zeros((cap,), hf.dtype).at[dest].set(flat_w[order].astype(hf.dtype))
    block_e = jnp.minimum(
        jnp.searchsorted(pend, jnp.arange(n_blocks) * MOE_BLOCK, side='right'), N_EXPERTS - 1)
    h_pad = jnp.concatenate([hf, jnp.zeros((1, D), hf.dtype)], axis=0)

    def expert_block(args):
        tok, wts, e = args
        xb = h_pad[tok]
        yb = (jax.nn.silu(xb @ w_gate[e]) * (xb @ w_up[e])) @ w_down[e]
        return yb * wts[:, None]

    y = lax.map(expert_block, (slot_tok.reshape(n_blocks, MOE_BLOCK),
                               slot_w.reshape(n_blocks, MOE_BLOCK), block_e))
    return jax.ops.segment_sum(y.reshape(cap, D), slot_tok, num_segments=T + 1)[:T]


def moe_ffn(h, w_router, b_router, w_gate, w_up, w_down, ws_gate, ws_up, ws_down):
    B, S, D = h.shape
    T = B * S
    hf = h.reshape(T, D)
    scores = jax.nn.sigmoid(jnp.matmul(hf, w_router, preferred_element_type=jnp.float32))
    sel = scores + b_router.astype(jnp.float32)
    per_group = N_EXPERTS // N_GROUPS
    grp_score = jnp.sum(lax.top_k(sel.reshape(T, N_GROUPS, per_group), 2)[0], axis=-1)
    _, grp_idx = lax.top_k(grp_score, TOPK_GROUPS)
    grp_mask = jnp.any(grp_idx[..., None] == jnp.arange(N_GROUPS), axis=1)
    masked = jnp.where(jnp.repeat(grp_mask, per_group, axis=1), sel, -jnp.inf)
    _, top_idx = lax.top_k(masked, TOP_K)
    top_w = jnp.take_along_axis(scores, top_idx, axis=1)
    top_w = top_w / (jnp.sum(top_w, axis=-1, keepdims=True) + 1e-20) * ROUTED_SCALE
    routed = routed_experts(hf, top_idx, top_w, w_gate, w_up, w_down)
    shared = swiglu(hf, ws_gate, ws_up, ws_down)
    return (routed + shared).reshape(B, S, D)


def setup_inputs(seed: int = 0) -> dict:
    key = jax.random.key(seed)
    keys = iter(jax.random.split(key, 64))
    L = DEPTH
    Lv = DEPTH - 1
    D = D_MODEL

    def nrm(shape, scale):
        return jax.random.normal(next(keys), shape, jnp.float32) * scale

    def uni(shape, lo, hi):
        return jax.random.uniform(next(keys), shape, jnp.float32, lo, hi)

    return {
        "x": nrm((BATCH, SEQ, D), 1.0),
        "c": nrm((BATCH, D), 1.0),
        "w_ada": nrm((L, D, 6 * D), 0.5 * D ** -0.5),
        "b_ada": nrm((L, 6 * D), 0.02),
        "w_in": nrm((L, D, IN_COLS), D ** -0.5),
        "w_vres": nrm((Lv, D, RW_VRES_RANK), D ** -0.5),
        "mu_rw": uni((L, RW_COLS), 0.0, 1.0),
        "mu_vres": uni((Lv, RW_VRES_RANK), 0.0, 1.0),
        "rw_w0": uni((L, RW_WIDTH), -4.0, 1.0),
        "rw_w2": nrm((L, RW_DECAY_RANK, RW_WIDTH), 0.1 * RW_DECAY_RANK ** -0.5),
        "rw_a0": nrm((L, RW_WIDTH), 0.1),
        "rw_a2": nrm((L, RW_ICLR_RANK, RW_WIDTH), 0.5 * RW_ICLR_RANK ** -0.5),
        "rw_v0": nrm((Lv, RW_WIDTH), 0.5),
        "rw_v2": nrm((Lv, RW_VRES_RANK, RW_WIDTH), 0.5 * RW_VRES_RANK ** -0.5),
        "rw_g2": nrm((L, RW_GATE_RANK, RW_WIDTH), RW_GATE_RANK ** -0.5),
        "rw_k_k": 0.85 + nrm((L, RW_WIDTH), 0.05),
        "rw_k_a": 1.0 + nrm((L, RW_WIDTH), 0.05),
        "rw_r_k": nrm((L, RW_HEADS, RW_HEAD_DIM), 0.1),
        "rw_ln_g": 1.0 + nrm((L, RW_WIDTH), 0.05),
        "rw_ln_b": nrm((L, RW_WIDTH), 0.02),
        "sb_norm_g": 1.0 + nrm((L, SB_WIDTH), 0.05),
        "pool_w": nrm((L, POOL_GROUPS, POOL_GROUP_DIM, POOL_GROUP_DIM), POOL_GROUP_DIM ** -0.5),
        "pool_scale": uni((L, POOL_WIDTH), 0.5, 1.5),
        "w_out": nrm((L, MIX_WIDTH, D), MIX_WIDTH ** -0.5),
        "w_router": nrm((L, D, N_EXPERTS), D ** -0.5),
        "b_router": nrm((L, N_EXPERTS), 0.01),
        "w_exp_gate": nrm((L, N_EXPERTS, D, EXPERT_DIM), D ** -0.5),
        "w_exp_up": nrm((L, N_EXPERTS, D, EXPERT_DIM), D ** -0.5),
        "w_exp_down": nrm((L, N_EXPERTS, EXPERT_DIM, D), EXPERT_DIM ** -0.5),
        "w_sh_gate": nrm((L, D, EXPERT_DIM), D ** -0.5),
        "w_sh_up": nrm((L, D, EXPERT_DIM), D ** -0.5),
        "w_sh_down": nrm((L, EXPERT_DIM, D), EXPERT_DIM ** -0.5),
        "g_final": 1.0 + nrm((D,), 0.05),
    }


def reference(x, c, w_ada, b_ada, w_in, w_vres, mu_rw, mu_vres, rw_w0, rw_w2, rw_a0, rw_a2,
              rw_v0, rw_v2, rw_g2, rw_k_k, rw_k_a, rw_r_k, rw_ln_g, rw_ln_b, sb_norm_g,
              pool_w, pool_scale, w_out, w_router, b_router, w_exp_gate, w_exp_up, w_exp_down,
              w_sh_gate, w_sh_up, w_sh_down, g_final):
    B, S, D = x.shape
    cond = jax.nn.silu(c)
    v_first = None
    for l in range(DEPTH):
        mod = cond @ w_ada[l] + b_ada[l]
        sh_m, sc_m, gt_m, sh_f, sc_f, gt_f = (m[:, None, :] for m in jnp.split(mod, 6, axis=-1))

        h = rms_norm(x) * (1.0 + sc_m) + sh_m
        w_in_l = w_in[l] if l == 0 else jnp.concatenate([w_in[l], w_vres[l - 1]], axis=1)
        proj = h @ w_in_l
        p_rw = token_shift(proj[..., :RW_COLS], mu_rw[l])
        p_sb = proj[..., RW_COLS:RW_COLS + SB_COLS]
        p_pool = proj[..., RW_COLS + SB_COLS:IN_COLS]

        if l == 0:
            v_gate = None
        else:
            p_vres = token_shift(proj[..., IN_COLS:], mu_vres[l - 1]).astype(jnp.float32)
            v_gate = jax.nn.sigmoid(rw_v0[l - 1] + p_vres @ rw_v2[l - 1])
        rw_out, v_l = rwkv7_time_mix(p_rw, v_first, v_gate, rw_w0[l], rw_w2[l], rw_a0[l],
                                     rw_a2[l], rw_g2[l], rw_k_k[l], rw_k_a[l], rw_r_k[l],
                                     rw_ln_g[l], rw_ln_b[l])
        if l == 0:
            v_first = v_l

        q_sb = split_heads(p_sb[..., :SB_WIDTH], SB_HEADS, SB_HEAD_DIM)
        k_sb = split_heads(p_sb[..., SB_WIDTH:2 * SB_WIDTH], SB_HEADS, SB_HEAD_DIM)
        v_sb = split_heads(p_sb[..., 2 * SB_WIDTH:], SB_HEADS, SB_HEAD_DIM)
        sb_out = stick_breaking_attention(q_sb, k_sb, v_sb)
        sb_out = (rms_norm(sb_out) * sb_norm_g[l].reshape(SB_HEADS, SB_HEAD_DIM)).reshape(B, S, SB_WIDTH)

        pool_out = multiscale_pool(p_pool, pool_w[l], pool_scale[l])

        mixed = jnp.concatenate([rw_out.astype(x.dtype), sb_out.astype(x.dtype), pool_out], axis=-1)
        x = x + gt_m * (mixed @ w_out[l])

        h = rms_norm(x) * (1.0 + sc_f) + sh_f
        ffn = moe_ffn(h, w_router[l], b_router[l], w_exp_gate[l], w_exp_up[l], w_exp_down[l],
                      w_sh_gate[l], w_sh_up[l], w_sh_down[l])
        x = x + gt_f * ffn.astype(x.dtype)

    return rms_norm(x) * g_final
```

```python
import functools

import jax
import jax.numpy as jnp
from jax import lax
from jax.experimental import pallas as pl
from jax.experimental.pallas import tpu as pltpu

F32 = jnp.float32
BF16 = jnp.bfloat16

HEAD_DIM = 64
RW_HEADS = 6
RW_WIDTH = RW_HEADS * HEAD_DIM
SB_HEADS = 6
SB_WIDTH = SB_HEADS * HEAD_DIM
POOL_WINDOWS = (2, 4, 8, 16)
POOL_WIDTH = len(POOL_WINDOWS) * HEAD_DIM
RW_DECAY_RANK = 32
RW_ICLR_RANK = 32
RW_VRES_RANK = 32
RW_GATE_RANK = 64
RW_LORA = RW_DECAY_RANK + RW_ICLR_RANK + RW_GATE_RANK
RW_COLS = 3 * RW_WIDTH + RW_LORA
SB_COLS = 3 * SB_WIDTH
IN_COLS = RW_COLS + SB_COLS + POOL_WIDTH
RW_GN_EPS = 64e-5
RMS_EPS = 1e-6
N_GROUPS = 8
TOPK_GROUPS = 4
TOP_K = 6
ROUTED_SCALE = 2.5

LANES = 128
RW_CHUNK = 64
RW_SUB = 16
VMEM_LIMIT = 48 * 1024 * 1024

HIGHEST = lax.Precision.HIGHEST


def _dot(a, b):
    return jnp.dot(a, b, preferred_element_type=F32)


def _bdot(a, b):
    return jnp.dot(a.astype(BF16), b.astype(BF16), preferred_element_type=F32)


def _bdot_nt(a, b):
    return lax.dot_general(a.astype(BF16), b.astype(BF16), (((1,), (1,)), ((), ())),
                           preferred_element_type=F32)


def _bdot_tn(a, b):
    return lax.dot_general(a.astype(BF16), b.astype(BF16), (((0,), (0,)), ((), ())),
                           preferred_element_type=F32)


def _split_dot(x, w, terms, left=False):
    acc = None
    rem = x
    for t in range(terms):
        part = rem.astype(BF16)
        d = _dot(w, part) if left else _dot(part, w)
        acc = d if acc is None else acc + d
        if t + 1 < terms:
            rem = rem - part.astype(F32)
    return acc


def _sigmoid(x):
    return 1.0 / (1.0 + jnp.exp(-x))


def _softplus(x):
    return jnp.maximum(x, 0.0) + jnp.log(1.0 + jnp.exp(-jnp.abs(x)))


def _params(*sem):
    return pltpu.CompilerParams(dimension_semantics=sem, vmem_limit_bytes=VMEM_LIMIT)


def _ada_kernel(c_ref, w_ref, b_ref, o_ref):
    c = c_ref[...]
    cond = c * _sigmoid(c)
    o_ref[...] = jnp.dot(cond, w_ref[...], precision=HIGHEST,
                         preferred_element_type=F32) + b_ref[...]


def _ada(c, w_ada, b_ada):
    L, D, N = w_ada.shape
    B = c.shape[0]
    tn = 1536 if N % 1536 == 0 else N
    return pl.pallas_call(
        _ada_kernel,
        grid=(L, N // tn),
        in_specs=[pl.BlockSpec((B, D), lambda l, n: (0, 0)),
                  pl.BlockSpec((None, D, tn), lambda l, n: (l, 0, n)),
                  pl.BlockSpec((None, 1, tn), lambda l, n: (l, 0, n))],
        out_specs=pl.BlockSpec((None, B, tn), lambda l, n: (l, 0, n)),
        out_shape=jax.ShapeDtypeStruct((L, B, N), F32),
        compiler_params=_params("arbitrary", "arbitrary"),
        name="ada_mod",
    )(c, w_ada, b_ada.reshape(L, 1, N))


def _mod_spec(l, j, tiles_per_batch, D):
    return pl.BlockSpec((None, None, None, 1, D),
                        lambda i, *_: (l, i // tiles_per_batch, j, 0, 0))


def _inproj_kernel(x_ref, sh_ref, sc_ref, w_ref, prw_ref, qkv_ref, pool_ref, *vres_ref):
    x = x_ref[...]
    ms = jnp.mean(x * x, axis=-1, keepdims=True)
    h = x * lax.rsqrt(ms + RMS_EPS) * (1.0 + sc_ref[...]) + sh_ref[...]
    hb = h.astype(BF16)
    prw_ref[...] = _dot(hb, w_ref[:, 0:RW_COLS])
    qkv_ref[...] = _dot(hb, w_ref[:, RW_COLS:RW_COLS + SB_COLS]).astype(BF16)
    pool_ref[...] = _dot(hb, w_ref[:, RW_COLS + SB_COLS:IN_COLS])
    if vres_ref:
        vres_ref[0][...] = _dot(hb, w_ref[:, IN_COLS:IN_COLS + LANES])


def _inproj(x2, mod5, l, w_bf, S, has_vres):
    T, D = x2.shape
    tm = min(512, S)
    tpb = S // tm
    n_w = w_bf.shape[1]
    out_shape = [jax.ShapeDtypeStruct((T, RW_COLS), F32),
                 jax.ShapeDtypeStruct((T, SB_COLS), BF16),
                 jax.ShapeDtypeStruct((T, POOL_WIDTH), F32)]
    out_specs = [pl.BlockSpec((tm, RW_COLS), lambda i: (i, 0)),
                 pl.BlockSpec((tm, SB_COLS), lambda i: (i, 0)),
                 pl.BlockSpec((tm, POOL_WIDTH), lambda i: (i, 0))]
    if has_vres:
        out_shape.append(jax.ShapeDtypeStruct((T, LANES), F32))
        out_specs.append(pl.BlockSpec((tm, LANES), lambda i: (i, 0)))
    return pl.pallas_call(
        _inproj_kernel,
        grid=(T // tm,),
        in_specs=[pl.BlockSpec((tm, D), lambda i: (i, 0)),
                  _mod_spec(l, 0, tpb, D), _mod_spec(l, 1, tpb, D),
                  pl.BlockSpec((D, n_w), lambda i: (0, 0))],
        out_specs=out_specs,
        out_shape=out_shape,
        compiler_params=_params("arbitrary"),
        name="in_proj",
    )(x2, mod5, mod5, w_bf)


def _unit_lower_solve(a_strict, rhs, eye, sub_mask):
    ad = jnp.where(sub_mask, a_strict, 0.0)
    ao = a_strict - ad
    inv_d = eye + ad
    pw = ad
    n = 2
    while n < RW_SUB:
        pw = _bdot(pw, pw)
        inv_d = inv_d + _bdot(inv_d, pw)
        n *= 2
    m = _bdot(inv_d, ao)
    t = _bdot(inv_d, rhs)
    terms = []
    n = 1
    pw = m
    while n < RW_CHUNK // RW_SUB:
        terms.append(pw)
        n *= 2
        if n < RW_CHUNK // RW_SUB:
            pw = _bdot(pw, pw)
    for pw in reversed(terms):
        t = t + _bdot(pw, t)
    return t


def _rwkv_kernel(*refs, has_vres, tm):
    if has_vres:
        (prw_ref, pvr_ref, vfirst_ref, mu_ref, w0_ref, w2_ref, a0_ref, a2_ref, g2_ref,
         kk_ref, ka_ref, rk_ref, lng_ref, lnb_ref, hblk_ref, ltri_ref,
         muv_ref, v0_ref, v2_ref,
         out_ref,
         carry_ref, carryv_ref, state_ref, at_ref, rt_ref, bt_ref, kt_ref, v_ref, cum_ref,
         y_ref, g_ref, bonus_ref) = refs
        vout_ref = None
    else:
        (prw_ref, mu_ref, w0_ref, w2_ref, a0_ref, a2_ref, g2_ref,
         kk_ref, ka_ref, rk_ref, lng_ref, lnb_ref, hblk_ref, ltri_ref,
         out_ref, vout_ref,
         carry_ref, state_ref, at_ref, rt_ref, bt_ref, kt_ref, v_ref, cum_ref,
         y_ref, g_ref, bonus_ref) = refs
        pvr_ref = vfirst_ref = muv_ref = v0_ref = v2_ref = carryv_ref = None

    s = pl.program_id(1)

    @pl.when(s == 0)
    def _():
        carry_ref[...] = jnp.zeros_like(carry_ref)
        state_ref[...] = jnp.zeros_like(state_ref)
        if has_vres:
            carryv_ref[...] = jnp.zeros_like(carryv_ref)

    row0 = lax.broadcasted_iota(jnp.int32, (tm, 1), 0) == 0

    def shift(p, c_ref, mu):
        prev = jnp.where(row0, c_ref[0:1, :], pltpu.roll(p, 1, 0))
        c_ref[0:1, :] = p[tm - 1:tm, :]
        return p + (prev - p) * mu

    W = RW_WIDTH
    f = shift(prw_ref[...], carry_ref, mu_ref[...])
    r = f[:, 0:W]
    k = f[:, W:2 * W]
    v = f[:, 2 * W:3 * W]
    lora = f[:, 3 * W:3 * W + RW_LORA]

    wlin = w0_ref[...] + _bdot(jnp.tanh(lora), w2_ref[...])
    lw = -jnp.exp(-_softplus(-wlin) - 0.5)
    a = _sigmoid(a0_ref[...] + _bdot(lora, a2_ref[...]))
    g_ref[...] = _bdot(_sigmoid(lora), g2_ref[...])
    if has_vres:
        fv = shift(pvr_ref[...], carryv_ref, muv_ref[...])
        v_gate = _sigmoid(v0_ref[...] + _bdot(fv, v2_ref[...]))
        v = v + (vfirst_ref[...] - v) * v_gate
    else:
        vout_ref[...] = v

    hblk = hblk_ref[...]
    kk = k * kk_ref[...]
    kk = kk / jnp.maximum(jnp.sqrt(_split_dot(kk * kk, hblk, 2)), 1e-12)
    k = k * (1.0 + (a - 1.0) * ka_ref[...])
    bonus_ref[...] = _split_dot(r * k * rk_ref[...], hblk, 2) * v

    cum = _split_dot(lw, ltri_ref[...], 3, left=True)
    p_inc = jnp.exp(cum)
    inv = jnp.exp(-cum)
    at_ref[...] = -kk * jnp.exp(cum - lw)
    rt_ref[...] = r * p_inc
    bt_ref[...] = kk * a * inv
    kt_ref[...] = k * inv
    v_ref[...] = v
    cum_ref[...] = cum

    C = RW_CHUNK
    ri = lax.broadcasted_iota(jnp.int32, (C, C), 0)
    ci = lax.broadcasted_iota(jnp.int32, (C, C), 1)
    strict = ri > ci
    incl = ri >= ci
    eye = (ri == ci).astype(F32)
    sub_mask = (ri // RW_SUB) == (ci // RW_SUB)

    def chunk(c, carry):
        r0 = pl.multiple_of(c * C, C)
        rows = pl.ds(r0, C)
        tail = pl.ds(pl.multiple_of(r0 + C - 8, 8), 8)
        for h in range(RW_HEADS):
            ls = slice(h * HEAD_DIM, (h + 1) * HEAD_DIM)
            at = at_ref[rows, ls]
            rt = rt_ref[rows, ls]
            bt = bt_ref[rows, ls]
            kt = kt_ref[rows, ls]
            vv = v_ref[rows, ls]
            pc = jnp.exp(cum_ref[tail, ls][7:8, :])
            st = state_ref[h]
            a_ab = jnp.where(strict, _bdot_nt(at, bt), 0.0)
            a_ak = jnp.where(strict, _bdot_nt(at, kt), 0.0)
            a_rb = jnp.where(incl, _bdot_nt(rt, bt), 0.0)
            a_rk = jnp.where(incl, _bdot_nt(rt, kt), 0.0)
            rhs = _bdot_nt(at, st) + _bdot(a_ak, vv)
            z = _unit_lower_solve(a_ab, rhs, eye, sub_mask)
            y_ref[rows, ls] = _bdot_nt(rt, st) + _bdot(a_rb, z) + _bdot(a_rk, vv)
            state_ref[h] = st * pc + _bdot_tn(z, bt * pc) + _bdot_tn(vv, kt * pc)
        return carry

    lax.fori_loop(0, tm // C, chunk, 0)

    y = y_ref[...]
    inv_n = 1.0 / HEAD_DIM
    mean = _split_dot(y, hblk, 2) * inv_n
    d = y - mean
    var = _split_dot(d * d, hblk, 2) * inv_n
    yn = d * lax.rsqrt(var + RW_GN_EPS) * lng_ref[...] + lnb_ref[...]
    out_ref[...] = ((yn + bonus_ref[...]) * g_ref[...]).astype(out_ref.dtype)


def _rwkv(prw, pvres, v_first, p, B, S):
    T = prw.shape[0]
    tm = min(512, S)
    ns = S // tm
    has_vres = pvres is not None
    W = RW_WIDTH
    row = lambda b, s: (b * ns + s, 0)
    const = lambda b, s: (0, 0)
    vec = lambda n: pl.BlockSpec((1, n), const)
    mat = lambda m, n: pl.BlockSpec((m, n), const)

    hblk = jnp.kron(jnp.eye(RW_HEADS, dtype=F32), jnp.ones((HEAD_DIM, HEAD_DIM), F32)).astype(BF16)
    ltri = jnp.kron(jnp.eye(tm // RW_CHUNK, dtype=F32),
                    jnp.tril(jnp.ones((RW_CHUNK, RW_CHUNK), F32))).astype(BF16)

    inputs = [prw]
    in_specs = [pl.BlockSpec((tm, RW_COLS), row)]
    if has_vres:
        inputs += [pvres, v_first]
        in_specs += [pl.BlockSpec((tm, LANES), row), pl.BlockSpec((tm, W), row)]
    inputs += [p["mu"], p["w0"], p["w2"], p["a0"], p["a2"], p["g2"], p["k_k"], p["k_a"],
               p["r_k"], p["ln_g"], p["ln_b"], hblk, ltri]
    in_specs += [vec(RW_COLS), vec(W), mat(RW_LORA, W), vec(W), mat(RW_LORA, W), mat(RW_LORA, W),
                 vec(W), vec(W), vec(W), vec(W), vec(W), mat(W, W), mat(tm, tm)]
    if has_vres:
        inputs += [p["mu_v"], p["v0"], p["v2"]]
        in_specs += [vec(LANES), vec(W), mat(LANES, W)]

    out_shape = [jax.ShapeDtypeStruct((T, W), BF16)]
    out_specs = [pl.BlockSpec((tm, W), row)]
    if not has_vres:
        out_shape.append(jax.ShapeDtypeStruct((T, W), F32))
        out_specs.append(pl.BlockSpec((tm, W), row))

    big = lambda: pltpu.VMEM((tm, W), F32)
    scratch = [pltpu.VMEM((8, RW_COLS), F32)]
    if has_vres:
        scratch.append(pltpu.VMEM((8, LANES), F32))
    scratch += [pltpu.VMEM((RW_HEADS, HEAD_DIM, HEAD_DIM), F32)] + [big() for _ in range(9)]

    res = pl.pallas_call(
        functools.partial(_rwkv_kernel, has_vres=has_vres, tm=tm),
        grid=(B, ns),
        in_specs=in_specs,
        out_specs=out_specs,
        out_shape=out_shape,
        scratch_shapes=scratch,
        compiler_params=_params("arbitrary", "arbitrary"),
        name="rwkv7_mix",
    )(*inputs)
    return (res[0], v_first) if has_vres else (res[0], res[1])


def _sb_kernel(q_ref, k_ref, v_ref, g_ref, o_ref, *, tq):
    i = pl.program_id(2)
    lane = lax.broadcasted_iota(jnp.int32, (1, LANES), 1)
    head_of_lane = lane // HEAD_DIM
    ri = lax.broadcasted_iota(jnp.int32, (tq, tq), 0)
    ci = lax.broadcasted_iota(jnp.int32, (tq, tq), 1)
    diff = ri - ci
    u_incl = (ri >= ci).astype(BF16)
    scale = HEAD_DIM ** -0.5
    q_all = q_ref[...]

    accs = []
    for h in range(2):
        qh = jnp.where(head_of_lane == h, q_all, jnp.zeros_like(q_all)) * jnp.asarray(scale, BF16)

        def body(jj, carry, qh=qh):
            acc, run = carry
            j = i - jj
            k0 = pl.multiple_of(j * tq, tq)
            kt = k_ref[pl.ds(k0, tq), :]
            vt = v_ref[pl.ds(k0, tq), :]
            z = lax.dot_general(qh, kt, (((1,), (1,)), ((), ())), preferred_element_type=F32)
            valid = diff > (j - i) * tq
            ln = -_softplus(z)
            lnm = jnp.where(valid, ln, 0.0)
            inc = _split_dot(lnm, u_incl, 2)
            att = jnp.where(valid, jnp.exp(z + ln + (inc - lnm) + run), 0.0)
            acc = acc + _dot(att.astype(BF16), vt)
            return acc, run + inc[:, 0:1]

        acc, _ = lax.fori_loop(0, i + 1, body,
                               (jnp.zeros((tq, LANES), F32), jnp.zeros((tq, 1), F32)))
        accs.append(acc)

    o = jnp.where(head_of_lane == 0, accs[0], accs[1])
    o2 = o * o
    s0 = jnp.sum(jnp.where(head_of_lane == 0, o2, 0.0), axis=-1, keepdims=True)
    s1 = jnp.sum(o2, axis=-1, keepdims=True) - s0
    ms = jnp.where(head_of_lane == 0, s0, s1) * (1.0 / HEAD_DIM)
    o_ref[...] = (o * lax.rsqrt(ms + RMS_EPS) * g_ref[...]).astype(o_ref.dtype)


def _sb_attention(qkv, g, B, S):
    T = qkv.shape[0]
    tq = min(256, S)
    nq = S // tq
    npair = SB_WIDTH // LANES
    return pl.pallas_call(
        functools.partial(_sb_kernel, tq=tq),
        grid=(B, npair, nq),
        in_specs=[pl.BlockSpec((tq, LANES), lambda b, p, i: (b * nq + i, p)),
                  pl.BlockSpec((S, LANES), lambda b, p, i: (b, npair + p)),
                  pl.BlockSpec((S, LANES), lambda b, p, i: (b, 2 * npair + p)),
                  pl.BlockSpec((1, LANES), lambda b, p, i: (0, p))],
        out_specs=pl.BlockSpec((tq, LANES), lambda b, p, i: (b * nq + i, p)),
        out_shape=jax.ShapeDtypeStruct((T, SB_WIDTH), BF16),
        compiler_params=_params("arbitrary", "arbitrary", "arbitrary"),
        name="stickbreak_attn",
    )(qkv, qkv, qkv, g)


def _pool_kernel(x_ref, w_ref, sc_ref, o_ref, *, S):
    x = x_ref[...]
    pos = lax.broadcasted_iota(jnp.int32, (S, 1), 0)
    group = lax.broadcasted_iota(jnp.int32, (1, POOL_WIDTH), 1) // HEAD_DIM

    sums = []
    acc = x
    step = 1
    for _ in POOL_WINDOWS:
        acc = acc + jnp.where(pos >= step, pltpu.roll(acc, step, 0), 0.0)
        step *= 2
        sums.append(acc)
    sel = sums[-1]
    win = jnp.full((1, POOL_WIDTH), float(POOL_WINDOWS[-1]), F32)
    for gi in range(len(POOL_WINDOWS) - 2, -1, -1):
        sel = jnp.where(group == gi, sums[gi], sel)
        win = jnp.where(group == gi, float(POOL_WINDOWS[gi]), win)
    count = jnp.minimum((pos + 1).astype(F32), win)
    pooled = sel / count - x
    o_ref[...] = (_bdot(pooled, w_ref[...]) * sc_ref[...]).astype(o_ref.dtype)


def _pool(ppool, w_blk, scale, B, S):
    T = ppool.shape[0]
    return pl.pallas_call(
        functools.partial(_pool_kernel, S=S),
        grid=(B,),
        in_specs=[pl.BlockSpec((S, POOL_WIDTH), lambda b: (b, 0)),
                  pl.BlockSpec((POOL_WIDTH, POOL_WIDTH), lambda b: (0, 0)),
                  pl.BlockSpec((1, POOL_WIDTH), lambda b: (0, 0))],
        out_specs=pl.BlockSpec((S, POOL_WIDTH), lambda b: (b, 0)),
        out_shape=jax.ShapeDtypeStruct((T, POOL_WIDTH), BF16),
        compiler_params=_params("arbitrary"),
        name="multiscale_pool",
    )(ppool, w_blk, scale)


def _outproj_kernel(rw_ref, sb_ref, pool_ref, x_ref, gt_ref, sh_ref, sc_ref, w_ref, wr_ref,
                    xo_ref, h_ref, lg_ref):
    a = RW_WIDTH
    b = RW_WIDTH + SB_WIDTH
    mixed = (_dot(rw_ref[...], w_ref[0:a, :]) + _dot(sb_ref[...], w_ref[a:b, :])
             + _dot(pool_ref[...], w_ref[b:, :]))
    x = x_ref[...] + gt_ref[...] * mixed
    xo_ref[...] = x
    ms = jnp.mean(x * x, axis=-1, keepdims=True)
    h = x * lax.rsqrt(ms + RMS_EPS) * (1.0 + sc_ref[...]) + sh_ref[...]
    h_ref[...] = h.astype(BF16)
    lg_ref[...] = jnp.dot(h, wr_ref[...], precision=HIGHEST, preferred_element_type=F32)


def _outproj(rw, sb, pool, x2, mod5, l, w_bf, wr_pad, S):
    T, D = x2.shape
    tm = min(512, S)
    tpb = S // tm
    row = lambda n: pl.BlockSpec((tm, n), lambda i: (i, 0))
    return pl.pallas_call(
        _outproj_kernel,
        grid=(T // tm,),
        in_specs=[row(RW_WIDTH), row(SB_WIDTH), row(POOL_WIDTH), row(D),
                  _mod_spec(l, 2, tpb, D), _mod_spec(l, 3, tpb, D), _mod_spec(l, 4, tpb, D),
                  pl.BlockSpec((D, D), lambda i: (0, 0)),
                  pl.BlockSpec((D, LANES), lambda i: (0, 0))],
        out_specs=[row(D), row(D), row(LANES)],
        out_shape=[jax.ShapeDtypeStruct((T, D), F32), jax.ShapeDtypeStruct((T, D), BF16),
                   jax.ShapeDtypeStruct((T, LANES), F32)],
        compiler_params=_params("arbitrary"),
        name="out_proj",
    )(rw, sb, pool, x2, mod5, mod5, mod5, w_bf, wr_pad)


def _router_kernel(lg_ref, b_ref, gate_ref, *, n_exp, tm):
    per_group = n_exp // N_GROUPS
    neg = -jnp.inf
    lt = lg_ref[...].T[0:n_exp, :]
    scores = _sigmoid(lt)
    sel = scores + b_ref[...]
    sel3 = sel.reshape(N_GROUPS, per_group, tm)
    i_in = lax.broadcasted_iota(jnp.int32, sel3.shape, 1)
    m1 = jnp.max(sel3, axis=1, keepdims=True)
    first = jnp.min(jnp.where(sel3 == m1, i_in, per_group), axis=1, keepdims=True)
    m2 = jnp.max(jnp.where(i_in == first, neg, sel3), axis=1, keepdims=True)
    gs = m1 + m2
    gi = lax.broadcasted_iota(jnp.int32, gs.shape, 0)
    grank = jnp.zeros(gs.shape, jnp.int32)
    for g2 in range(N_GROUPS):
        o = gs[g2:g2 + 1]
        beats = jnp.where(o > gs, 1, jnp.where(o == gs, jnp.where(gi > g2, 1, 0), 0))
        grank = grank + beats
    masked = jnp.where(grank < TOPK_GROUPS, sel3, neg).reshape(n_exp, tm)
    ei = lax.broadcasted_iota(jnp.int32, masked.shape, 0)
    rank = jnp.zeros(masked.shape, jnp.int32)
    for e2 in range(n_exp):
        o = masked[e2:e2 + 1, :]
        beats = jnp.where(o > masked, 1, jnp.where(o == masked, jnp.where(ei > e2, 1, 0), 0))
        rank = rank + beats
    w = jnp.where(rank < TOP_K, scores, 0.0)
    denom = jnp.sum(w, axis=0, keepdims=True)
    gate = w / (denom + 1e-20) * ROUTED_SCALE
    gate = jnp.concatenate([gate, jnp.zeros((LANES - n_exp, tm), F32)], axis=0)
    gate_ref[...] = gate.T


def _router(logits, b_col, n_exp):
    T = logits.shape[0]
    tm = min(512, T)
    return pl.pallas_call(
        functools.partial(_router_kernel, n_exp=n_exp, tm=tm),
        grid=(T // tm,),
        in_specs=[pl.BlockSpec((tm, LANES), lambda i: (i, 0)),
                  pl.BlockSpec((n_exp, 1), lambda i: (0, 0))],
        out_specs=pl.BlockSpec((tm, LANES), lambda i: (i, 0)),
        out_shape=jax.ShapeDtypeStruct((T, LANES), F32),
        compiler_params=_params("arbitrary"),
        name="router_topk",
    )(logits, b_col)


def _experts_kernel(h_ref, gate_ref, x_ref, gt_ref, wg_ref, wu_ref, wd_ref,
                    sg_ref, su_ref, sd_ref, gfin_ref, o_ref, *, n_exp, final_norm):
    e = pl.program_id(1)

    @pl.when(e == 0)
    def _():
        o_ref[...] = jnp.zeros_like(o_ref)

    def ffn(wg, wu, wd, gcol):
        h = h_ref[...]
        a = _dot(h, wg[...].astype(BF16))
        u = _dot(h, wu[...].astype(BF16))
        act = a * _sigmoid(a) * u
        if gcol is not None:
            act = act * gcol
        o_ref[...] += _dot(act.astype(BF16), wd[...].astype(BF16))

    @pl.when(e < n_exp)
    def _():
        lane = lax.broadcasted_iota(jnp.int32, (1, LANES), 1)
        gcol = jnp.sum(jnp.where(lane == e, gate_ref[...], 0.0), axis=-1, keepdims=True)
        ffn(wg_ref, wu_ref, wd_ref, gcol)

    @pl.when(e == n_exp)
    def _():
        ffn(sg_ref, su_ref, sd_ref, None)
        x = x_ref[...] + gt_ref[...] * o_ref[...]
        if final_norm:
            ms = jnp.mean(x * x, axis=-1, keepdims=True)
            x = x * lax.rsqrt(ms + RMS_EPS) * gfin_ref[...]
        o_ref[...] = x


def _experts(h2, gate, x2, mod5, l, w_gate, w_up, w_down, s_gate, s_up, s_down, g_final, S,
             final_norm):
    T, D = x2.shape
    n_exp, _, F = w_gate.shape[1:]
    tm = min(1024, S)
    tpb = S // tm
    ex = lambda e: jnp.minimum(e, n_exp - 1)
    return pl.pallas_call(
        functools.partial(_experts_kernel, n_exp=n_exp, final_norm=final_norm),
        grid=(T // tm, n_exp + 1),
        in_specs=[pl.BlockSpec((tm, D), lambda i, e: (i, 0)),
                  pl.BlockSpec((tm, LANES), lambda i, e: (i, 0)),
                  pl.BlockSpec((tm, D), lambda i, e: (i, 0)),
                  _mod_spec(l, 5, tpb, D),
                  pl.BlockSpec((None, None, D, F), lambda i, e: (l, ex(e), 0, 0)),
                  pl.BlockSpec((None, None, D, F), lambda i, e: (l, ex(e), 0, 0)),
                  pl.BlockSpec((None, None, F, D), lambda i, e: (l, ex(e), 0, 0)),
                  pl.BlockSpec((None, D, F), lambda i, e: (l, 0, 0)),
                  pl.BlockSpec((None, D, F), lambda i, e: (l, 0, 0)),
                  pl.BlockSpec((None, F, D), lambda i, e: (l, 0, 0)),
                  pl.BlockSpec((1, D), lambda i, e: (0, 0))],
        out_specs=pl.BlockSpec((tm, D), lambda i, e: (i, 0)),
        out_shape=jax.ShapeDtypeStruct((T, D), F32),
        compiler_params=_params("arbitrary", "arbitrary"),
        name="experts_ffn",
    )(h2, gate, x2, mod5, w_gate, w_up, w_down, s_gate, s_up, s_down, g_final)


def _pad_rows(w, lo, n_rows):
    return jnp.zeros((n_rows, w.shape[1]), w.dtype).at[lo:lo + w.shape[0]].set(w)


def kernel(x, c, w_ada, b_ada, w_in, w_vres, mu_rw, mu_vres, rw_w0, rw_w2, rw_a0, rw_a2, rw_v0, rw_v2, rw_g2, rw_k_k, rw_k_a, rw_r_k, rw_ln_g, rw_ln_b, sb_norm_g, pool_w, pool_scale, w_out, w_router, b_router, w_exp_gate, w_exp_up, w_exp_down, w_sh_gate, w_sh_up, w_sh_down, g_final):
    B, S, D = x.shape
    L = w_in.shape[0]
    T = B * S
    n_exp = w_router.shape[2]
    W = RW_WIDTH

    mod = _ada(c, w_ada, b_ada)
    mod5 = mod.reshape(L, B, 6, 1, D)

    x2 = x.reshape(T, D)
    v_first = None
    for l in range(L):
        has_vres = l > 0
        w_l = w_in[l]
        if has_vres:
            pad = jnp.zeros((D, LANES - RW_VRES_RANK), F32)
            w_l = jnp.concatenate([w_l, w_vres[l - 1], pad], axis=1)
        proj = _inproj(x2, mod5, l, w_l.astype(BF16), S, has_vres)
        prw, qkv, ppool = proj[0], proj[1], proj[2]
        pvres = proj[3] if has_vres else None

        o1 = 0
        o2 = RW_DECAY_RANK
        o3 = RW_DECAY_RANK + RW_ICLR_RANK
        rp = {
            "mu": mu_rw[l].reshape(1, RW_COLS),
            "w0": rw_w0[l].reshape(1, W),
            "w2": _pad_rows(rw_w2[l], o1, RW_LORA).astype(BF16),
            "a0": rw_a0[l].reshape(1, W),
            "a2": _pad_rows(rw_a2[l], o2, RW_LORA).astype(BF16),
            "g2": _pad_rows(rw_g2[l], o3, RW_LORA).astype(BF16),
            "k_k": rw_k_k[l].reshape(1, W),
            "k_a": rw_k_a[l].reshape(1, W),
            "r_k": rw_r_k[l].reshape(1, W),
            "ln_g": rw_ln_g[l].reshape(1, W),
            "ln_b": rw_ln_b[l].reshape(1, W),
        }
        if has_vres:
            rp["mu_v"] = jnp.concatenate(
                [mu_vres[l - 1], jnp.zeros((LANES - RW_VRES_RANK,), F32)]).reshape(1, LANES)
            rp["v0"] = rw_v0[l - 1].reshape(1, W)
            rp["v2"] = _pad_rows(rw_v2[l - 1], 0, LANES).astype(BF16)
        rw_out, v_first = _rwkv(prw, pvres, v_first, rp, B, S)

        sb_out = _sb_attention(qkv, sb_norm_g[l].reshape(1, SB_WIDTH), B, S)

        n_pool = len(POOL_WINDOWS)
        w_blk = (jnp.eye(n_pool, dtype=F32)[:, None, :, None] * pool_w[l][:, :, None, :]
                 ).reshape(POOL_WIDTH, POOL_WIDTH).astype(BF16)
        pool_out = _pool(ppool, w_blk, pool_scale[l].reshape(1, POOL_WIDTH), B, S)

        wr_pad = jnp.concatenate([w_router[l], jnp.zeros((D, LANES - n_exp), F32)], axis=1)
        x2, h2, logits = _outproj(rw_out, sb_out, pool_out, x2, mod5, l, w_out[l].astype(BF16),
                                  wr_pad, S)
        gate = _router(logits, b_router[l].reshape(n_exp, 1), n_exp)
        x2 = _experts(h2, gate, x2, mod5, l, w_exp_gate, w_exp_up, w_exp_down,
                      w_sh_gate, w_sh_up, w_sh_down, g_final.reshape(1, D), S,
                      final_norm=(l == L - 1))
    return x2.reshape(B, S, D)
```

```python
import functools

import jax
import jax.numpy as jnp
from jax import lax
from jax.experimental import pallas as pl
from jax.experimental.pallas import tpu as pltpu

F32 = jnp.float32
BF16 = jnp.bfloat16

HEAD_DIM = 64
RW_HEADS = 6
RW_WIDTH = RW_HEADS * HEAD_DIM
SB_HEADS = 6
SB_WIDTH = SB_HEADS * HEAD_DIM
POOL_WINDOWS = (2, 4, 8, 16)
POOL_WIDTH = len(POOL_WINDOWS) * HEAD_DIM
RW_DECAY_RANK = 32
RW_ICLR_RANK = 32
RW_VRES_RANK = 32
RW_GATE_RANK = 64
RW_LORA = RW_DECAY_RANK + RW_ICLR_RANK + RW_GATE_RANK
RW_COLS = 3 * RW_WIDTH + RW_LORA
SB_COLS = 3 * SB_WIDTH
IN_COLS = RW_COLS + SB_COLS + POOL_WIDTH
RW_GN_EPS = 64e-5
RMS_EPS = 1e-6
N_GROUPS = 8
TOPK_GROUPS = 4
TOP_K = 6
ROUTED_SCALE = 2.5

LANES = 128
RW_CHUNK = 64
RW_SUB = 16
VMEM_LIMIT = 48 * 1024 * 1024
SB_DEAD_LOG = -104.0

HIGHEST = lax.Precision.HIGHEST


def _dot(a, b):
    return jnp.dot(a, b, preferred_element_type=F32)


def _bdot(a, b):
    return jnp.dot(a.astype(BF16), b.astype(BF16), preferred_element_type=F32)


def _bdot_nt(a, b):
    return lax.dot_general(a.astype(BF16), b.astype(BF16), (((1,), (1,)), ((), ())),
                           preferred_element_type=F32)


def _bdot_tn(a, b):
    return lax.dot_general(a.astype(BF16), b.astype(BF16), (((0,), (0,)), ((), ())),
                           preferred_element_type=F32)


def _split_dot(x, w, terms, left=False):
    acc = None
    rem = x
    for t in range(terms):
        part = rem.astype(BF16)
        d = _dot(w, part) if left else _dot(part, w)
        acc = d if acc is None else acc + d
        if t + 1 < terms:
            rem = rem - part.astype(F32)
    return acc


def _sigmoid(x):
    return 1.0 / (1.0 + jnp.exp(-x))


def _softplus(x):
    return jnp.maximum(x, 0.0) + jnp.log(1.0 + jnp.exp(-jnp.abs(x)))


def _params(*sem):
    return pltpu.CompilerParams(dimension_semantics=sem, vmem_limit_bytes=VMEM_LIMIT)


def _ada_kernel(c_ref, w_ref, b_ref, o_ref):
    c = c_ref[...]
    cond = c * _sigmoid(c)
    o_ref[...] = jnp.dot(cond, w_ref[...], precision=HIGHEST,
                         preferred_element_type=F32) + b_ref[...]


def _ada(c, w_ada, b_ada):
    L, D, N = w_ada.shape
    B = c.shape[0]
    tn = 1536 if N % 1536 == 0 else N
    return pl.pallas_call(
        _ada_kernel,
        grid=(L, N // tn),
        in_specs=[pl.BlockSpec((B, D), lambda l, n: (0, 0)),
                  pl.BlockSpec((None, D, tn), lambda l, n: (l, 0, n)),
                  pl.BlockSpec((None, 1, tn), lambda l, n: (l, 0, n))],
        out_specs=pl.BlockSpec((None, B, tn), lambda l, n: (l, 0, n)),
        out_shape=jax.ShapeDtypeStruct((L, B, N), F32),
        compiler_params=_params("arbitrary", "arbitrary"),
        name="ada_mod",
    )(c, w_ada, b_ada.reshape(L, 1, N))


def _mod_spec(l, j, tiles_per_batch, D):
    return pl.BlockSpec((None, None, None, 1, D),
                        lambda i, *_: (l, i // tiles_per_batch, j, 0, 0))


def _inproj_kernel(x_ref, sh_ref, sc_ref, w_ref, prw_ref, qkv_ref, pool_ref, *vres_ref):
    x = x_ref[...]
    ms = jnp.mean(x * x, axis=-1, keepdims=True)
    h = x * lax.rsqrt(ms + RMS_EPS) * (1.0 + sc_ref[...]) + sh_ref[...]
    hb = h.astype(BF16)
    prw_ref[...] = _dot(hb, w_ref[:, 0:RW_COLS])
    qkv_ref[...] = _dot(hb, w_ref[:, RW_COLS:RW_COLS + SB_COLS]).astype(BF16)
    pool_ref[...] = _dot(hb, w_ref[:, RW_COLS + SB_COLS:IN_COLS])
    if vres_ref:
        vres_ref[0][...] = _dot(hb, w_ref[:, IN_COLS:IN_COLS + LANES])


def _inproj(x2, mod5, l, w_bf, S, has_vres):
    T, D = x2.shape
    tm = min(512, S)
    tpb = S // tm
    n_w = w_bf.shape[1]
    out_shape = [jax.ShapeDtypeStruct((T, RW_COLS), F32),
                 jax.ShapeDtypeStruct((T, SB_COLS), BF16),
                 jax.ShapeDtypeStruct((T, POOL_WIDTH), F32)]
    out_specs = [pl.BlockSpec((tm, RW_COLS), lambda i: (i, 0)),
                 pl.BlockSpec((tm, SB_COLS), lambda i: (i, 0)),
                 pl.BlockSpec((tm, POOL_WIDTH), lambda i: (i, 0))]
    if has_vres:
        out_shape.append(jax.ShapeDtypeStruct((T, LANES), F32))
        out_specs.append(pl.BlockSpec((tm, LANES), lambda i: (i, 0)))
    return pl.pallas_call(
        _inproj_kernel,
        grid=(T // tm,),
        in_specs=[pl.BlockSpec((tm, D), lambda i: (i, 0)),
                  _mod_spec(l, 0, tpb, D), _mod_spec(l, 1, tpb, D),
                  pl.BlockSpec((D, n_w), lambda i: (0, 0))],
        out_specs=out_specs,
        out_shape=out_shape,
        compiler_params=_params("arbitrary"),
        name="in_proj",
    )(x2, mod5, mod5, w_bf)


def _unit_lower_solve(a_strict, rhs, eye, sub_mask):
    n_p = range(len(a_strict))
    ad = [jnp.where(sub_mask, a, 0.0) for a in a_strict]
    ao = [a_strict[i] - ad[i] for i in n_p]
    inv_d = [eye + ad[i] for i in n_p]
    pw = ad
    n = 2
    while n < RW_SUB:
        pw = [_bdot(pw[i], pw[i]) for i in n_p]
        inv_d = [inv_d[i] + _bdot(inv_d[i], pw[i]) for i in n_p]
        n *= 2
    m = [_bdot(inv_d[i], ao[i]) for i in n_p]
    t = [_bdot(inv_d[i], rhs[i]) for i in n_p]
    terms = []
    n = 1
    pw = m
    while n < RW_CHUNK // RW_SUB:
        terms.append(pw)
        n *= 2
        if n < RW_CHUNK // RW_SUB:
            pw = [_bdot(pw[i], pw[i]) for i in n_p]
    for pw in reversed(terms):
        t = [t[i] + _bdot(pw[i], t[i]) for i in n_p]
    return t


def _rwkv_kernel(*refs, has_vres, tm):
    if has_vres:
        (prw_ref, pvr_ref, vfirst_ref, mu_ref, w0_ref, w2_ref, a0_ref, a2_ref, g2_ref,
         kk_ref, ka_ref, rk_ref, lng_ref, lnb_ref, hblk_ref, ltri_ref,
         muv_ref, v0_ref, v2_ref,
         out_ref,
         carry_ref, carryv_ref, state_ref, at_ref, rt_ref, bt_ref, kt_ref, v_ref, cum_ref,
         y_ref, g_ref, bonus_ref) = refs
        vout_ref = None
    else:
        (prw_ref, mu_ref, w0_ref, w2_ref, a0_ref, a2_ref, g2_ref,
         kk_ref, ka_ref, rk_ref, lng_ref, lnb_ref, hblk_ref, ltri_ref,
         out_ref, vout_ref,
         carry_ref, state_ref, at_ref, rt_ref, bt_ref, kt_ref, v_ref, cum_ref,
         y_ref, g_ref, bonus_ref) = refs
        pvr_ref = vfirst_ref = muv_ref = v0_ref = v2_ref = carryv_ref = None

    s = pl.program_id(1)

    @pl.when(s == 0)
    def _():
        carry_ref[...] = jnp.zeros_like(carry_ref)
        state_ref[...] = jnp.zeros_like(state_ref)
        if has_vres:
            carryv_ref[...] = jnp.zeros_like(carryv_ref)

    row0 = lax.broadcasted_iota(jnp.int32, (tm, 1), 0) == 0

    def shift(p, c_ref, mu):
        prev = jnp.where(row0, c_ref[0:1, :], pltpu.roll(p, 1, 0))
        c_ref[0:1, :] = p[tm - 1:tm, :]
        return p + (prev - p) * mu

    W = RW_WIDTH
    f = shift(prw_ref[...], carry_ref, mu_ref[...])
    r = f[:, 0:W]
    k = f[:, W:2 * W]
    v = f[:, 2 * W:3 * W]
    lora = f[:, 3 * W:3 * W + RW_LORA]

    wlin = w0_ref[...] + _bdot(jnp.tanh(lora), w2_ref[...])
    lw = -jnp.exp(-_softplus(-wlin) - 0.5)
    a = _sigmoid(a0_ref[...] + _bdot(lora, a2_ref[...]))
    g_ref[...] = _bdot(_sigmoid(lora), g2_ref[...])
    if has_vres:
        fv = shift(pvr_ref[...], carryv_ref, muv_ref[...])
        v_gate = _sigmoid(v0_ref[...] + _bdot(fv, v2_ref[...]))
        v = v + (vfirst_ref[...] - v) * v_gate
    else:
        vout_ref[...] = v

    hblk = hblk_ref[...]
    kk = k * kk_ref[...]
    kk = kk / jnp.maximum(jnp.sqrt(_split_dot(kk * kk, hblk, 2)), 1e-12)
    k = k * (1.0 + (a - 1.0) * ka_ref[...])
    bonus_ref[...] = _split_dot(r * k * rk_ref[...], hblk, 2) * v

    cum = _split_dot(lw, ltri_ref[...], 3, left=True)
    p_inc = jnp.exp(cum)
    inv = jnp.exp(-cum)
    at_ref[...] = -kk * jnp.exp(cum - lw)
    rt_ref[...] = r * p_inc
    bt_ref[...] = kk * a * inv
    kt_ref[...] = k * inv
    v_ref[...] = v
    cum_ref[...] = cum

    C = RW_CHUNK
    ri = lax.broadcasted_iota(jnp.int32, (C, C), 0)
    ci = lax.broadcasted_iota(jnp.int32, (C, C), 1)
    strict = ri > ci
    incl = ri >= ci
    eye = (ri == ci).astype(F32)
    sub_mask = (ri // RW_SUB) == (ci // RW_SUB)

    def chunk(c, carry):
        r0 = pl.multiple_of(c * C, C)
        rows = pl.ds(r0, C)
        tail = pl.ds(pl.multiple_of(r0 + C - 8, 8), 8)
        hs = range(RW_HEADS)
        ls = [slice(h * HEAD_DIM, (h + 1) * HEAD_DIM) for h in hs]
        at = [at_ref[rows, ls[h]] for h in hs]
        rt = [rt_ref[rows, ls[h]] for h in hs]
        bt = [bt_ref[rows, ls[h]] for h in hs]
        kt = [kt_ref[rows, ls[h]] for h in hs]
        vv = [v_ref[rows, ls[h]] for h in hs]
        pc = [jnp.exp(cum_ref[tail, ls[h]][7:8, :]) for h in hs]
        st = [state_ref[h] for h in hs]
        a_ab = [jnp.where(strict, _bdot_nt(at[h], bt[h]), 0.0) for h in hs]
        a_ak = [jnp.where(strict, _bdot_nt(at[h], kt[h]), 0.0) for h in hs]
        a_rb = [jnp.where(incl, _bdot_nt(rt[h], bt[h]), 0.0) for h in hs]
        a_rk = [jnp.where(incl, _bdot_nt(rt[h], kt[h]), 0.0) for h in hs]
        a_s = [_bdot_nt(at[h], st[h]) for h in hs]
        r_s = [_bdot_nt(rt[h], st[h]) for h in hs]
        rhs = [a_s[h] + _bdot(a_ak[h], vv[h]) for h in hs]
        y0 = [r_s[h] + _bdot(a_rk[h], vv[h]) for h in hs]
        s0 = [st[h] * pc[h] + _bdot_tn(vv[h], kt[h] * pc[h]) for h in hs]
        z = _unit_lower_solve(a_ab, rhs, eye, sub_mask)
        y = [y0[h] + _bdot(a_rb[h], z[h]) for h in hs]
        s1 = [s0[h] + _bdot_tn(z[h], bt[h] * pc[h]) for h in hs]
        for h in hs:
            y_ref[rows, ls[h]] = y[h]
            state_ref[h] = s1[h]
        return carry

    lax.fori_loop(0, tm // C, chunk, 0)

    y = y_ref[...]
    inv_n = 1.0 / HEAD_DIM
    mean = _split_dot(y, hblk, 2) * inv_n
    d = y - mean
    var = _split_dot(d * d, hblk, 2) * inv_n
    yn = d * lax.rsqrt(var + RW_GN_EPS) * lng_ref[...] + lnb_ref[...]
    out_ref[...] = ((yn + bonus_ref[...]) * g_ref[...]).astype(out_ref.dtype)


def _rwkv(prw, pvres, v_first, p, B, S):
    T = prw.shape[0]
    tm = min(512, S)
    ns = S // tm
    has_vres = pvres is not None
    W = RW_WIDTH
    row = lambda b, s: (b * ns + s, 0)
    const = lambda b, s: (0, 0)
    vec = lambda n: pl.BlockSpec((1, n), const)
    mat = lambda m, n: pl.BlockSpec((m, n), const)

    hblk = jnp.kron(jnp.eye(RW_HEADS, dtype=F32), jnp.ones((HEAD_DIM, HEAD_DIM), F32)).astype(BF16)
    ltri = jnp.kron(jnp.eye(tm // RW_CHUNK, dtype=F32),
                    jnp.tril(jnp.ones((RW_CHUNK, RW_CHUNK), F32))).astype(BF16)

    inputs = [prw]
    in_specs = [pl.BlockSpec((tm, RW_COLS), row)]
    if has_vres:
        inputs += [pvres, v_first]
        in_specs += [pl.BlockSpec((tm, LANES), row), pl.BlockSpec((tm, W), row)]
    inputs += [p["mu"], p["w0"], p["w2"], p["a0"], p["a2"], p["g2"], p["k_k"], p["k_a"],
               p["r_k"], p["ln_g"], p["ln_b"], hblk, ltri]
    in_specs += [vec(RW_COLS), vec(W), mat(RW_LORA, W), vec(W), mat(RW_LORA, W), mat(RW_LORA, W),
                 vec(W), vec(W), vec(W), vec(W), vec(W), mat(W, W), mat(tm, tm)]
    if has_vres:
        inputs += [p["mu_v"], p["v0"], p["v2"]]
        in_specs += [vec(LANES), vec(W), mat(LANES, W)]

    out_shape = [jax.ShapeDtypeStruct((T, W), BF16)]
    out_specs = [pl.BlockSpec((tm, W), row)]
    if not has_vres:
        out_shape.append(jax.ShapeDtypeStruct((T, W), F32))
        out_specs.append(pl.BlockSpec((tm, W), row))

    big = lambda: pltpu.VMEM((tm, W), F32)
    scratch = [pltpu.VMEM((8, RW_COLS), F32)]
    if has_vres:
        scratch.append(pltpu.VMEM((8, LANES), F32))
    scratch += [pltpu.VMEM((RW_HEADS, HEAD_DIM, HEAD_DIM), F32)] + [big() for _ in range(9)]

    res = pl.pallas_call(
        functools.partial(_rwkv_kernel, has_vres=has_vres, tm=tm),
        grid=(B, ns),
        in_specs=in_specs,
        out_specs=out_specs,
        out_shape=out_shape,
        scratch_shapes=scratch,
        compiler_params=_params("arbitrary", "arbitrary"),
        name="rwkv7_mix",
    )(*inputs)
    return (res[0], v_first) if has_vres else (res[0], res[1])


def _sb_kernel(q_ref, k_ref, v_ref, g_ref, o_ref, *, tq):
    i = pl.program_id(2)
    lane = lax.broadcasted_iota(jnp.int32, (1, LANES), 1)
    head_of_lane = lane // HEAD_DIM
    ri = lax.broadcasted_iota(jnp.int32, (tq, tq), 0)
    ci = lax.broadcasted_iota(jnp.int32, (tq, tq), 1)
    causal = ri > ci
    u_incl = (ri >= ci).astype(BF16)
    scale = HEAD_DIM ** -0.5
    q_all = q_ref[...]
    heads = range(2)
    qh = [jnp.where(head_of_lane == h, q_all, jnp.zeros_like(q_all)) * jnp.asarray(scale, BF16)
          for h in heads]

    def tile(j, acc, run, diagonal):
        k0 = pl.multiple_of(j * tq, tq)
        kt = k_ref[pl.ds(k0, tq), :]
        vt = v_ref[pl.ds(k0, tq), :]
        z = [lax.dot_general(qh[h], kt, (((1,), (1,)), ((), ())), preferred_element_type=F32)
             for h in heads]
        ln = [-_softplus(z[h]) for h in heads]
        lnm = [jnp.where(causal, ln[h], 0.0) for h in heads] if diagonal else ln
        inc = [_split_dot(lnm[h], u_incl, 2) for h in heads]
        att = [jnp.exp(z[h] + ln[h] + (inc[h] - lnm[h]) + run[h]) for h in heads]
        if diagonal:
            att = [jnp.where(causal, att[h], 0.0) for h in heads]
        acc = [acc[h] + _dot(att[h].astype(BF16), vt) for h in heads]
        run = [run[h] + inc[h][:, 0:1] for h in heads]
        return acc, run

    acc, run = tile(i, [jnp.zeros((tq, LANES), F32)] * 2, [jnp.zeros((tq, 1), F32)] * 2, True)

    def alive(run):
        return jnp.max(jnp.maximum(run[0], run[1]))

    def cond(carry):
        j, _, _, top = carry
        return jnp.logical_and(j >= 0, top > SB_DEAD_LOG)

    def body(carry):
        j, acc, run, _ = carry
        acc, run = tile(j, acc, run, False)
        return j - 1, acc, run, alive(run)

    _, accs, _, _ = lax.while_loop(cond, body, (i - 1, acc, run, alive(run)))

    o = jnp.where(head_of_lane == 0, accs[0], accs[1])
    o2 = o * o
    s0 = jnp.sum(jnp.where(head_of_lane == 0, o2, 0.0), axis=-1, keepdims=True)
    s1 = jnp.sum(o2, axis=-1, keepdims=True) - s0
    ms = jnp.where(head_of_lane == 0, s0, s1) * (1.0 / HEAD_DIM)
    o_ref[...] = (o * lax.rsqrt(ms + RMS_EPS) * g_ref[...]).astype(o_ref.dtype)


def _sb_attention(qkv, g, B, S):
    T = qkv.shape[0]
    tq = min(256, S)
    nq = S // tq
    npair = SB_WIDTH // LANES
    return pl.pallas_call(
        functools.partial(_sb_kernel, tq=tq),
        grid=(B, npair, nq),
        in_specs=[pl.BlockSpec((tq, LANES), lambda b, p, i: (b * nq + i, p)),
                  pl.BlockSpec((S, LANES), lambda b, p, i: (b, npair + p)),
                  pl.BlockSpec((S, LANES), lambda b, p, i: (b, 2 * npair + p)),
                  pl.BlockSpec((1, LANES), lambda b, p, i: (0, p))],
        out_specs=pl.BlockSpec((tq, LANES), lambda b, p, i: (b * nq + i, p)),
        out_shape=jax.ShapeDtypeStruct((T, SB_WIDTH), BF16),
        compiler_params=_params("arbitrary", "arbitrary", "arbitrary"),
        name="stickbreak_attn",
    )(qkv, qkv, qkv, g)


def _pool_kernel(x_ref, w_ref, sc_ref, o_ref, *, S):
    x = x_ref[...]
    pos = lax.broadcasted_iota(jnp.int32, (S, 1), 0)
    group = lax.broadcasted_iota(jnp.int32, (1, POOL_WIDTH), 1) // HEAD_DIM

    sums = []
    acc = x
    step = 1
    for _ in POOL_WINDOWS:
        acc = acc + jnp.where(pos >= step, pltpu.roll(acc, step, 0), 0.0)
        step *= 2
        sums.append(acc)
    sel = sums[-1]
    win = jnp.full((1, POOL_WIDTH), float(POOL_WINDOWS[-1]), F32)
    for gi in range(len(POOL_WINDOWS) - 2, -1, -1):
        sel = jnp.where(group == gi, sums[gi], sel)
        win = jnp.where(group == gi, float(POOL_WINDOWS[gi]), win)
    count = jnp.minimum((pos + 1).astype(F32), win)
    pooled = sel / count - x
    o_ref[...] = (_bdot(pooled, w_ref[...]) * sc_ref[...]).astype(o_ref.dtype)


def _pool(ppool, w_blk, scale, B, S):
    T = ppool.shape[0]
    return pl.pallas_call(
        functools.partial(_pool_kernel, S=S),
        grid=(B,),
        in_specs=[pl.BlockSpec((S, POOL_WIDTH), lambda b: (b, 0)),
                  pl.BlockSpec((POOL_WIDTH, POOL_WIDTH), lambda b: (0, 0)),
                  pl.BlockSpec((1, POOL_WIDTH), lambda b: (0, 0))],
        out_specs=pl.BlockSpec((S, POOL_WIDTH), lambda b: (b, 0)),
        out_shape=jax.ShapeDtypeStruct((T, POOL_WIDTH), BF16),
        compiler_params=_params("arbitrary"),
        name="multiscale_pool",
    )(ppool, w_blk, scale)


def _outproj_kernel(rw_ref, sb_ref, pool_ref, x_ref, gt_ref, sh_ref, sc_ref, w_ref, wr_ref,
                    xo_ref, h_ref, lg_ref):
    a = RW_WIDTH
    b = RW_WIDTH + SB_WIDTH
    mixed = (_dot(rw_ref[...], w_ref[0:a, :]) + _dot(sb_ref[...], w_ref[a:b, :])
             + _dot(pool_ref[...], w_ref[b:, :]))
    x = x_ref[...] + gt_ref[...] * mixed
    xo_ref[...] = x
    ms = jnp.mean(x * x, axis=-1, keepdims=True)
    h = x * lax.rsqrt(ms + RMS_EPS) * (1.0 + sc_ref[...]) + sh_ref[...]
    h_ref[...] = h.astype(BF16)
    lg_ref[...] = jnp.dot(h, wr_ref[...], precision=HIGHEST, preferred_element_type=F32)


def _outproj(rw, sb, pool, x2, mod5, l, w_bf, wr_pad, S):
    T, D = x2.shape
    tm = min(512, S)
    tpb = S // tm
    row = lambda n: pl.BlockSpec((tm, n), lambda i: (i, 0))
    return pl.pallas_call(
        _outproj_kernel,
        grid=(T // tm,),
        in_specs=[row(RW_WIDTH), row(SB_WIDTH), row(POOL_WIDTH), row(D),
                  _mod_spec(l, 2, tpb, D), _mod_spec(l, 3, tpb, D), _mod_spec(l, 4, tpb, D),
                  pl.BlockSpec((D, D), lambda i: (0, 0)),
                  pl.BlockSpec((D, LANES), lambda i: (0, 0))],
        out_specs=[row(D), row(D), row(LANES)],
        out_shape=[jax.ShapeDtypeStruct((T, D), F32), jax.ShapeDtypeStruct((T, D), BF16),
                   jax.ShapeDtypeStruct((T, LANES), F32)],
        compiler_params=_params("arbitrary"),
        name="out_proj",
    )(rw, sb, pool, x2, mod5, mod5, mod5, w_bf, wr_pad)


def _router_kernel(lg_ref, b_ref, gate_ref, *, n_exp, tm):
    per_group = n_exp // N_GROUPS
    neg = -jnp.inf
    lt = lg_ref[...].T[0:n_exp, :]
    scores = _sigmoid(lt)
    sel = scores + b_ref[...]
    sel3 = sel.reshape(N_GROUPS, per_group, tm)
    i_in = lax.broadcasted_iota(jnp.int32, sel3.shape, 1)
    m1 = jnp.max(sel3, axis=1, keepdims=True)
    first = jnp.min(jnp.where(sel3 == m1, i_in, per_group), axis=1, keepdims=True)
    m2 = jnp.max(jnp.where(i_in == first, neg, sel3), axis=1, keepdims=True)
    gs = m1 + m2
    gi = lax.broadcasted_iota(jnp.int32, gs.shape, 0)
    grank = jnp.zeros(gs.shape, jnp.int32)
    for g2 in range(N_GROUPS):
        o = gs[g2:g2 + 1]
        beats = jnp.where(o > gs, 1, jnp.where(o == gs, jnp.where(gi > g2, 1, 0), 0))
        grank = grank + beats
    masked = jnp.where(grank < TOPK_GROUPS, sel3, neg).reshape(n_exp, tm)
    ei = lax.broadcasted_iota(jnp.int32, masked.shape, 0)
    rank = jnp.zeros(masked.shape, jnp.int32)
    for e2 in range(n_exp):
        o = masked[e2:e2 + 1, :]
        beats = jnp.where(o > masked, 1, jnp.where(o == masked, jnp.where(ei > e2, 1, 0), 0))
        rank = rank + beats
    w = jnp.where(rank < TOP_K, scores, 0.0)
    denom = jnp.sum(w, axis=0, keepdims=True)
    gate = w / (denom + 1e-20) * ROUTED_SCALE
    gate = jnp.concatenate([gate, jnp.zeros((LANES - n_exp, tm), F32)], axis=0)
    gate_ref[...] = gate.T


def _router(logits, b_col, n_exp):
    T = logits.shape[0]
    tm = min(512, T)
    return pl.pallas_call(
        functools.partial(_router_kernel, n_exp=n_exp, tm=tm),
        grid=(T // tm,),
        in_specs=[pl.BlockSpec((tm, LANES), lambda i: (i, 0)),
                  pl.BlockSpec((n_exp, 1), lambda i: (0, 0))],
        out_specs=pl.BlockSpec((tm, LANES), lambda i: (i, 0)),
        out_shape=jax.ShapeDtypeStruct((T, LANES), F32),
        compiler_params=_params("arbitrary"),
        name="router_topk",
    )(logits, b_col)


def _experts_kernel(h_ref, gate_ref, x_ref, gt_ref, wg_ref, wu_ref, wd_ref,
                    sg_ref, su_ref, sd_ref, gfin_ref, o_ref, *, n_exp, final_norm):
    e = pl.program_id(1)

    @pl.when(e == 0)
    def _():
        o_ref[...] = jnp.zeros_like(o_ref)

    def ffn(wg, wu, wd, gcol):
        h = h_ref[...]
        a = _dot(h, wg[...].astype(BF16))
        u = _dot(h, wu[...].astype(BF16))
        act = a * _sigmoid(a) * u
        if gcol is not None:
            act = act * gcol
        o_ref[...] += _dot(act.astype(BF16), wd[...].astype(BF16))

    @pl.when(e < n_exp)
    def _():
        lane = lax.broadcasted_iota(jnp.int32, (1, LANES), 1)
        gcol = jnp.sum(jnp.where(lane == e, gate_ref[...], 0.0), axis=-1, keepdims=True)
        ffn(wg_ref, wu_ref, wd_ref, gcol)

    @pl.when(e == n_exp)
    def _():
        ffn(sg_ref, su_ref, sd_ref, None)
        x = x_ref[...] + gt_ref[...] * o_ref[...]
        if final_norm:
            ms = jnp.mean(x * x, axis=-1, keepdims=True)
            x = x * lax.rsqrt(ms + RMS_EPS) * gfin_ref[...]
        o_ref[...] = x


def _experts(h2, gate, x2, mod5, l, w_gate, w_up, w_down, s_gate, s_up, s_down, g_final, S,
             final_norm):
    T, D = x2.shape
    n_exp, _, F = w_gate.shape[1:]
    tm = min(1024, S)
    tpb = S // tm
    ex = lambda e: jnp.minimum(e, n_exp - 1)
    return pl.pallas_call(
        functools.partial(_experts_kernel, n_exp=n_exp, final_norm=final_norm),
        grid=(T // tm, n_exp + 1),
        in_specs=[pl.BlockSpec((tm, D), lambda i, e: (i, 0)),
                  pl.BlockSpec((tm, LANES), lambda i, e: (i, 0)),
                  pl.BlockSpec((tm, D), lambda i, e: (i, 0)),
                  _mod_spec(l, 5, tpb, D),
                  pl.BlockSpec((None, None, D, F), lambda i, e: (l, ex(e), 0, 0)),
                  pl.BlockSpec((None, None, D, F), lambda i, e: (l, ex(e), 0, 0)),
                  pl.BlockSpec((None, None, F, D), lambda i, e: (l, ex(e), 0, 0)),
                  pl.BlockSpec((None, D, F), lambda i, e: (l, 0, 0)),
                  pl.BlockSpec((None, D, F), lambda i, e: (l, 0, 0)),
                  pl.BlockSpec((None, F, D), lambda i, e: (l, 0, 0)),
                  pl.BlockSpec((1, D), lambda i, e: (0, 0))],
        out_specs=pl.BlockSpec((tm, D), lambda i, e: (i, 0)),
        out_shape=jax.ShapeDtypeStruct((T, D), F32),
        compiler_params=_params("arbitrary", "arbitrary"),
        name="experts_ffn",
    )(h2, gate, x2, mod5, w_gate, w_up, w_down, s_gate, s_up, s_down, g_final)


def _pad_rows(w, lo, n_rows):
    return jnp.zeros((n_rows, w.shape[1]), w.dtype).at[lo:lo + w.shape[0]].set(w)


def kernel(x, c, w_ada, b_ada, w_in, w_vres, mu_rw, mu_vres, rw_w0, rw_w2, rw_a0, rw_a2, rw_v0, rw_v2, rw_g2, rw_k_k, rw_k_a, rw_r_k, rw_ln_g, rw_ln_b, sb_norm_g, pool_w, pool_scale, w_out, w_router, b_router, w_exp_gate, w_exp_up, w_exp_down, w_sh_gate, w_sh_up, w_sh_down, g_final):
    B, S, D = x.shape
    L = w_in.shape[0]
    T = B * S
    n_exp = w_router.shape[2]
    W = RW_WIDTH

    mod = _ada(c, w_ada, b_ada)
    mod5 = mod.reshape(L, B, 6, 1, D)

    x2 = x.reshape(T, D)
    v_first = None
    for l in range(L):
        has_vres = l > 0
        w_l = w_in[l]
        if has_vres:
            pad = jnp.zeros((D, LANES - RW_VRES_RANK), F32)
            w_l = jnp.concatenate([w_l, w_vres[l - 1], pad], axis=1)
        proj = _inproj(x2, mod5, l, w_l.astype(BF16), S, has_vres)
        prw, qkv, ppool = proj[0], proj[1], proj[2]
        pvres = proj[3] if has_vres else None

        o1 = 0
        o2 = RW_DECAY_RANK
        o3 = RW_DECAY_RANK + RW_ICLR_RANK
        rp = {
            "mu": mu_rw[l].reshape(1, RW_COLS),
            "w0": rw_w0[l].reshape(1, W),
            "w2": _pad_rows(rw_w2[l], o1, RW_LORA).astype(BF16),
            "a0": rw_a0[l].reshape(1, W),
            "a2": _pad_rows(rw_a2[l], o2, RW_LORA).astype(BF16),
            "g2": _pad_rows(rw_g2[l], o3, RW_LORA).astype(BF16),
            "k_k": rw_k_k[l].reshape(1, W),
            "k_a": rw_k_a[l].reshape(1, W),
            "r_k": rw_r_k[l].reshape(1, W),
            "ln_g": rw_ln_g[l].reshape(1, W),
            "ln_b": rw_ln_b[l].reshape(1, W),
        }
        if has_vres:
            rp["mu_v"] = jnp.concatenate(
                [mu_vres[l - 1], jnp.zeros((LANES - RW_VRES_RANK,), F32)]).reshape(1, LANES)
            rp["v0"] = rw_v0[l - 1].reshape(1, W)
            rp["v2"] = _pad_rows(rw_v2[l - 1], 0, LANES).astype(BF16)
        rw_out, v_first = _rwkv(prw, pvres, v_first, rp, B, S)

        sb_out = _sb_attention(qkv, sb_norm_g[l].reshape(1, SB_WIDTH), B, S)

        n_pool = len(POOL_WINDOWS)
        w_blk = (jnp.eye(n_pool, dtype=F32)[:, None, :, None] * pool_w[l][:, :, None, :]
                 ).reshape(POOL_WIDTH, POOL_WIDTH).astype(BF16)
        pool_out = _pool(ppool, w_blk, pool_scale[l].reshape(1, POOL_WIDTH), B, S)

        wr_pad = jnp.concatenate([w_router[l], jnp.zeros((D, LANES - n_exp), F32)], axis=1)
        x2, h2, logits = _outproj(rw_out, sb_out, pool_out, x2, mod5, l, w_out[l].astype(BF16),
                                  wr_pad, S)
        gate = _router(logits, b_router[l].reshape(n_exp, 1), n_exp)
        x2 = _experts(h2, gate, x2, mod5, l, w_exp_gate, w_exp_up, w_exp_down,
                      w_sh_gate, w_sh_up, w_sh_down, g_final.reshape(1, D), S,
                      final_norm=(l == L - 1))
    return x2.reshape(B, S, D)
```

```python
import functools

import jax
import jax.numpy as jnp
from jax import lax
from jax.experimental import pallas as pl
from jax.experimental.pallas import tpu as pltpu

F32 = jnp.float32
BF16 = jnp.bfloat16

HEAD_DIM = 64
RW_HEADS = 6
RW_WIDTH = RW_HEADS * HEAD_DIM
SB_HEADS = 6
SB_WIDTH = SB_HEADS * HEAD_DIM
POOL_WINDOWS = (2, 4, 8, 16)
POOL_WIDTH = len(POOL_WINDOWS) * HEAD_DIM
RW_DECAY_RANK = 32
RW_ICLR_RANK = 32
RW_VRES_RANK = 32
RW_GATE_RANK = 64
RW_LORA = RW_DECAY_RANK + RW_ICLR_RANK + RW_GATE_RANK
RW_COLS = 3 * RW_WIDTH + RW_LORA
SB_COLS = 3 * SB_WIDTH
IN_COLS = RW_COLS + SB_COLS + POOL_WIDTH
RW_GN_EPS = 64e-5
RMS_EPS = 1e-6
N_GROUPS = 8
TOPK_GROUPS = 4
TOP_K = 6
ROUTED_SCALE = 2.5

LANES = 128
RW_CHUNK = 64
RW_SUB = 16
VMEM_LIMIT = 48 * 1024 * 1024
MOE_GROUP = 8
MOE_SUB = 256
MOE_CAP = 48
SB_DEAD_LOG = -104.0

HIGHEST = lax.Precision.HIGHEST


def _dot(a, b):
    return jnp.dot(a, b, preferred_element_type=F32)


def _bdot(a, b):
    return jnp.dot(a.astype(BF16), b.astype(BF16), preferred_element_type=F32)


def _bdot_nt(a, b):
    return lax.dot_general(a.astype(BF16), b.astype(BF16), (((1,), (1,)), ((), ())),
                           preferred_element_type=F32)


def _bdot_tn(a, b):
    return lax.dot_general(a.astype(BF16), b.astype(BF16), (((0,), (0,)), ((), ())),
                           preferred_element_type=F32)


def _split_dot(x, w, terms, left=False):
    acc = None
    rem = x
    for t in range(terms):
        part = rem.astype(BF16)
        d = _dot(w, part) if left else _dot(part, w)
        acc = d if acc is None else acc + d
        if t + 1 < terms:
            rem = rem - part.astype(F32)
    return acc


def _sigmoid(x):
    return 1.0 / (1.0 + jnp.exp(-x))


def _softplus(x):
    return jnp.maximum(x, 0.0) + jnp.log(1.0 + jnp.exp(-jnp.abs(x)))


def _params(*sem):
    return pltpu.CompilerParams(dimension_semantics=sem, vmem_limit_bytes=VMEM_LIMIT)


def _ada_kernel(c_ref, w_ref, b_ref, o_ref):
    c = c_ref[...]
    cond = c * _sigmoid(c)
    o_ref[...] = jnp.dot(cond, w_ref[...], precision=HIGHEST,
                         preferred_element_type=F32) + b_ref[...]


def _ada(c, w_ada, b_ada):
    L, D, N = w_ada.shape
    B = c.shape[0]
    tn = 1536 if N % 1536 == 0 else N
    return pl.pallas_call(
        _ada_kernel,
        grid=(L, N // tn),
        in_specs=[pl.BlockSpec((B, D), lambda l, n: (0, 0)),
                  pl.BlockSpec((None, D, tn), lambda l, n: (l, 0, n)),
                  pl.BlockSpec((None, 1, tn), lambda l, n: (l, 0, n))],
        out_specs=pl.BlockSpec((None, B, tn), lambda l, n: (l, 0, n)),
        out_shape=jax.ShapeDtypeStruct((L, B, N), F32),
        compiler_params=_params("arbitrary", "arbitrary"),
        name="ada_mod",
    )(c, w_ada, b_ada.reshape(L, 1, N))


def _mod_spec(l, j, tiles_per_batch, D):
    return pl.BlockSpec((None, None, None, 1, D),
                        lambda i, *_: (l, i // tiles_per_batch, j, 0, 0))


def _inproj_kernel(x_ref, sh_ref, sc_ref, w_ref, prw_ref, qkv_ref, pool_ref, *vres_ref):
    x = x_ref[...]
    ms = jnp.mean(x * x, axis=-1, keepdims=True)
    h = x * lax.rsqrt(ms + RMS_EPS) * (1.0 + sc_ref[...]) + sh_ref[...]
    hb = h.astype(BF16)
    prw_ref[...] = _dot(hb, w_ref[:, 0:RW_COLS])
    qkv_ref[...] = _dot(hb, w_ref[:, RW_COLS:RW_COLS + SB_COLS]).astype(BF16)
    pool_ref[...] = _dot(hb, w_ref[:, RW_COLS + SB_COLS:IN_COLS])
    if vres_ref:
        vres_ref[0][...] = _dot(hb, w_ref[:, IN_COLS:IN_COLS + LANES])


def _inproj(x2, mod5, l, w_bf, S, has_vres):
    T, D = x2.shape
    tm = min(512, S)
    tpb = S // tm
    n_w = w_bf.shape[1]
    out_shape = [jax.ShapeDtypeStruct((T, RW_COLS), F32),
                 jax.ShapeDtypeStruct((T, SB_COLS), BF16),
                 jax.ShapeDtypeStruct((T, POOL_WIDTH), F32)]
    out_specs = [pl.BlockSpec((tm, RW_COLS), lambda i: (i, 0)),
                 pl.BlockSpec((tm, SB_COLS), lambda i: (i, 0)),
                 pl.BlockSpec((tm, POOL_WIDTH), lambda i: (i, 0))]
    if has_vres:
        out_shape.append(jax.ShapeDtypeStruct((T, LANES), F32))
        out_specs.append(pl.BlockSpec((tm, LANES), lambda i: (i, 0)))
    return pl.pallas_call(
        _inproj_kernel,
        grid=(T // tm,),
        in_specs=[pl.BlockSpec((tm, D), lambda i: (i, 0)),
                  _mod_spec(l, 0, tpb, D), _mod_spec(l, 1, tpb, D),
                  pl.BlockSpec((D, n_w), lambda i: (0, 0))],
        out_specs=out_specs,
        out_shape=out_shape,
        compiler_params=_params("arbitrary"),
        name="in_proj",
    )(x2, mod5, mod5, w_bf)


def _unit_lower_solve(a_strict, rhs, eye, sub_mask):
    n_p = range(len(a_strict))
    ad = [jnp.where(sub_mask, a, 0.0) for a in a_strict]
    ao = [a_strict[i] - ad[i] for i in n_p]
    inv_d = [eye + ad[i] for i in n_p]
    pw = ad
    n = 2
    while n < RW_SUB:
        pw = [_bdot(pw[i], pw[i]) for i in n_p]
        inv_d = [inv_d[i] + _bdot(inv_d[i], pw[i]) for i in n_p]
        n *= 2
    m = [_bdot(inv_d[i], ao[i]) for i in n_p]
    t = [_bdot(inv_d[i], rhs[i]) for i in n_p]
    terms = []
    n = 1
    pw = m
    while n < RW_CHUNK // RW_SUB:
        terms.append(pw)
        n *= 2
        if n < RW_CHUNK // RW_SUB:
            pw = [_bdot(pw[i], pw[i]) for i in n_p]
    for pw in reversed(terms):
        t = [t[i] + _bdot(pw[i], t[i]) for i in n_p]
    return t


def _rwkv_kernel(*refs, has_vres, tm):
    if has_vres:
        (prw_ref, pvr_ref, vfirst_ref, mu_ref, w0_ref, w2_ref, a0_ref, a2_ref, g2_ref,
         kk_ref, ka_ref, rk_ref, lng_ref, lnb_ref, hblk_ref, ltri_ref,
         muv_ref, v0_ref, v2_ref,
         out_ref,
         carry_ref, carryv_ref, state_ref, at_ref, rt_ref, bt_ref, kt_ref, v_ref, cum_ref,
         y_ref, g_ref, bonus_ref) = refs
        vout_ref = None
    else:
        (prw_ref, mu_ref, w0_ref, w2_ref, a0_ref, a2_ref, g2_ref,
         kk_ref, ka_ref, rk_ref, lng_ref, lnb_ref, hblk_ref, ltri_ref,
         out_ref, vout_ref,
         carry_ref, state_ref, at_ref, rt_ref, bt_ref, kt_ref, v_ref, cum_ref,
         y_ref, g_ref, bonus_ref) = refs
        pvr_ref = vfirst_ref = muv_ref = v0_ref = v2_ref = carryv_ref = None

    s = pl.program_id(1)

    @pl.when(s == 0)
    def _():
        carry_ref[...] = jnp.zeros_like(carry_ref)
        state_ref[...] = jnp.zeros_like(state_ref)
        if has_vres:
            carryv_ref[...] = jnp.zeros_like(carryv_ref)

    row0 = lax.broadcasted_iota(jnp.int32, (tm, 1), 0) == 0

    def shift(p, c_ref, mu):
        prev = jnp.where(row0, c_ref[0:1, :], pltpu.roll(p, 1, 0))
        c_ref[0:1, :] = p[tm - 1:tm, :]
        return p + (prev - p) * mu

    W = RW_WIDTH
    f = shift(prw_ref[...], carry_ref, mu_ref[...])
    r = f[:, 0:W]
    k = f[:, W:2 * W]
    v = f[:, 2 * W:3 * W]
    lora = f[:, 3 * W:3 * W + RW_LORA]

    wlin = w0_ref[...] + _bdot(jnp.tanh(lora), w2_ref[...])
    lw = -jnp.exp(-_softplus(-wlin) - 0.5)
    a = _sigmoid(a0_ref[...] + _bdot(lora, a2_ref[...]))
    g_ref[...] = _bdot(_sigmoid(lora), g2_ref[...])
    if has_vres:
        fv = shift(pvr_ref[...], carryv_ref, muv_ref[...])
        v_gate = _sigmoid(v0_ref[...] + _bdot(fv, v2_ref[...]))
        v = v + (vfirst_ref[...] - v) * v_gate
    else:
        vout_ref[...] = v

    hblk = hblk_ref[...]
    kk = k * kk_ref[...]
    kk = kk / jnp.maximum(jnp.sqrt(_split_dot(kk * kk, hblk, 2)), 1e-12)
    k = k * (1.0 + (a - 1.0) * ka_ref[...])
    bonus_ref[...] = _split_dot(r * k * rk_ref[...], hblk, 2) * v

    cum = _split_dot(lw, ltri_ref[...], 3, left=True)
    p_inc = jnp.exp(cum)
    inv = jnp.exp(-cum)
    at_ref[...] = -kk * jnp.exp(cum - lw)
    rt_ref[...] = r * p_inc
    bt_ref[...] = kk * a * inv
    kt_ref[...] = k * inv
    v_ref[...] = v
    cum_ref[...] = cum

    C = RW_CHUNK
    ri = lax.broadcasted_iota(jnp.int32, (C, C), 0)
    ci = lax.broadcasted_iota(jnp.int32, (C, C), 1)
    strict = ri > ci
    incl = ri >= ci
    eye = (ri == ci).astype(F32)
    sub_mask = (ri // RW_SUB) == (ci // RW_SUB)

    def chunk(c, carry):
        r0 = pl.multiple_of(c * C, C)
        rows = pl.ds(r0, C)
        tail = pl.ds(pl.multiple_of(r0 + C - 8, 8), 8)
        hs = range(RW_HEADS)
        ls = [slice(h * HEAD_DIM, (h + 1) * HEAD_DIM) for h in hs]
        at = [at_ref[rows, ls[h]] for h in hs]
        rt = [rt_ref[rows, ls[h]] for h in hs]
        bt = [bt_ref[rows, ls[h]] for h in hs]
        kt = [kt_ref[rows, ls[h]] for h in hs]
        vv = [v_ref[rows, ls[h]] for h in hs]
        pc = [jnp.exp(cum_ref[tail, ls[h]][7:8, :]) for h in hs]
        st = [state_ref[h] for h in hs]
        a_ab = [jnp.where(strict, _bdot_nt(at[h], bt[h]), 0.0) for h in hs]
        a_ak = [jnp.where(strict, _bdot_nt(at[h], kt[h]), 0.0) for h in hs]
        a_rb = [jnp.where(incl, _bdot_nt(rt[h], bt[h]), 0.0) for h in hs]
        a_rk = [jnp.where(incl, _bdot_nt(rt[h], kt[h]), 0.0) for h in hs]
        a_s = [_bdot_nt(at[h], st[h]) for h in hs]
        r_s = [_bdot_nt(rt[h], st[h]) for h in hs]
        rhs = [a_s[h] + _bdot(a_ak[h], vv[h]) for h in hs]
        y0 = [r_s[h] + _bdot(a_rk[h], vv[h]) for h in hs]
        s0 = [st[h] * pc[h] + _bdot_tn(vv[h], kt[h] * pc[h]) for h in hs]
        z = _unit_lower_solve(a_ab, rhs, eye, sub_mask)
        y = [y0[h] + _bdot(a_rb[h], z[h]) for h in hs]
        s1 = [s0[h] + _bdot_tn(z[h], bt[h] * pc[h]) for h in hs]
        for h in hs:
            y_ref[rows, ls[h]] = y[h]
            state_ref[h] = s1[h]
        return carry

    lax.fori_loop(0, tm // C, chunk, 0)

    y = y_ref[...]
    inv_n = 1.0 / HEAD_DIM
    mean = _split_dot(y, hblk, 2) * inv_n
    d = y - mean
    var = _split_dot(d * d, hblk, 2) * inv_n
    yn = d * lax.rsqrt(var + RW_GN_EPS) * lng_ref[...] + lnb_ref[...]
    out_ref[...] = ((yn + bonus_ref[...]) * g_ref[...]).astype(out_ref.dtype)


def _rwkv(prw, pvres, v_first, p, B, S):
    T = prw.shape[0]
    tm = min(512, S)
    ns = S // tm
    has_vres = pvres is not None
    W = RW_WIDTH
    row = lambda b, s: (b * ns + s, 0)
    const = lambda b, s: (0, 0)
    vec = lambda n: pl.BlockSpec((1, n), const)
    mat = lambda m, n: pl.BlockSpec((m, n), const)

    hblk = jnp.kron(jnp.eye(RW_HEADS, dtype=F32), jnp.ones((HEAD_DIM, HEAD_DIM), F32)).astype(BF16)
    ltri = jnp.kron(jnp.eye(tm // RW_CHUNK, dtype=F32),
                    jnp.tril(jnp.ones((RW_CHUNK, RW_CHUNK), F32))).astype(BF16)

    inputs = [prw]
    in_specs = [pl.BlockSpec((tm, RW_COLS), row)]
    if has_vres:
        inputs += [pvres, v_first]
        in_specs += [pl.BlockSpec((tm, LANES), row), pl.BlockSpec((tm, W), row)]
    inputs += [p["mu"], p["w0"], p["w2"], p["a0"], p["a2"], p["g2"], p["k_k"], p["k_a"],
               p["r_k"], p["ln_g"], p["ln_b"], hblk, ltri]
    in_specs += [vec(RW_COLS), vec(W), mat(RW_LORA, W), vec(W), mat(RW_LORA, W), mat(RW_LORA, W),
                 vec(W), vec(W), vec(W), vec(W), vec(W), mat(W, W), mat(tm, tm)]
    if has_vres:
        inputs += [p["mu_v"], p["v0"], p["v2"]]
        in_specs += [vec(LANES), vec(W), mat(LANES, W)]

    out_shape = [jax.ShapeDtypeStruct((T, W), BF16)]
    out_specs = [pl.BlockSpec((tm, W), row)]
    if not has_vres:
        out_shape.append(jax.ShapeDtypeStruct((T, W), F32))
        out_specs.append(pl.BlockSpec((tm, W), row))

    big = lambda: pltpu.VMEM((tm, W), F32)
    scratch = [pltpu.VMEM((8, RW_COLS), F32)]
    if has_vres:
        scratch.append(pltpu.VMEM((8, LANES), F32))
    scratch += [pltpu.VMEM((RW_HEADS, HEAD_DIM, HEAD_DIM), F32)] + [big() for _ in range(9)]

    res = pl.pallas_call(
        functools.partial(_rwkv_kernel, has_vres=has_vres, tm=tm),
        grid=(B, ns),
        in_specs=in_specs,
        out_specs=out_specs,
        out_shape=out_shape,
        scratch_shapes=scratch,
        compiler_params=_params("arbitrary", "arbitrary"),
        name="rwkv7_mix",
    )(*inputs)
    return (res[0], v_first) if has_vres else (res[0], res[1])


def _sb_kernel(q_ref, k_ref, v_ref, g_ref, o_ref, *, tq):
    i = pl.program_id(2)
    lane = lax.broadcasted_iota(jnp.int32, (1, LANES), 1)
    head_of_lane = lane // HEAD_DIM
    ri = lax.broadcasted_iota(jnp.int32, (tq, tq), 0)
    ci = lax.broadcasted_iota(jnp.int32, (tq, tq), 1)
    causal = ri > ci
    u_incl = (ri >= ci).astype(BF16)
    scale = HEAD_DIM ** -0.5
    q_all = q_ref[...]
    heads = range(2)
    qh = [jnp.where(head_of_lane == h, q_all, jnp.zeros_like(q_all)) * jnp.asarray(scale, BF16)
          for h in heads]

    def tile(j, acc, run, diagonal):
        k0 = pl.multiple_of(j * tq, tq)
        kt = k_ref[pl.ds(k0, tq), :]
        vt = v_ref[pl.ds(k0, tq), :]
        z = [lax.dot_general(qh[h], kt, (((1,), (1,)), ((), ())), preferred_element_type=F32)
             for h in heads]
        ln = [-_softplus(z[h]) for h in heads]
        lnm = [jnp.where(causal, ln[h], 0.0) for h in heads] if diagonal else ln
        inc = [_split_dot(lnm[h], u_incl, 2) for h in heads]
        att = [jnp.exp(z[h] + ln[h] + (inc[h] - lnm[h]) + run[h]) for h in heads]
        if diagonal:
            att = [jnp.where(causal, att[h], 0.0) for h in heads]
        acc = [acc[h] + _dot(att[h].astype(BF16), vt) for h in heads]
        run = [run[h] + inc[h][:, 0:1] for h in heads]
        return acc, run

    acc, run = tile(i, [jnp.zeros((tq, LANES), F32)] * 2, [jnp.zeros((tq, 1), F32)] * 2, True)

    def alive(run):
        return jnp.max(jnp.maximum(run[0], run[1]))

    def cond(carry):
        j, _, _, top = carry
        return jnp.logical_and(j >= 0, top > SB_DEAD_LOG)

    def body(carry):
        j, acc, run, _ = carry
        acc, run = tile(j, acc, run, False)
        return j - 1, acc, run, alive(run)

    _, accs, _, _ = lax.while_loop(cond, body, (i - 1, acc, run, alive(run)))

    o = jnp.where(head_of_lane == 0, accs[0], accs[1])
    o2 = o * o
    s0 = jnp.sum(jnp.where(head_of_lane == 0, o2, 0.0), axis=-1, keepdims=True)
    s1 = jnp.sum(o2, axis=-1, keepdims=True) - s0
    ms = jnp.where(head_of_lane == 0, s0, s1) * (1.0 / HEAD_DIM)
    o_ref[...] = (o * lax.rsqrt(ms + RMS_EPS) * g_ref[...]).astype(o_ref.dtype)


def _sb_attention(qkv, g, B, S):
    T = qkv.shape[0]
    tq = min(256, S)
    nq = S // tq
    npair = SB_WIDTH // LANES
    return pl.pallas_call(
        functools.partial(_sb_kernel, tq=tq),
        grid=(B, npair, nq),
        in_specs=[pl.BlockSpec((tq, LANES), lambda b, p, i: (b * nq + i, p)),
                  pl.BlockSpec((S, LANES), lambda b, p, i: (b, npair + p)),
                  pl.BlockSpec((S, LANES), lambda b, p, i: (b, 2 * npair + p)),
                  pl.BlockSpec((1, LANES), lambda b, p, i: (0, p))],
        out_specs=pl.BlockSpec((tq, LANES), lambda b, p, i: (b * nq + i, p)),
        out_shape=jax.ShapeDtypeStruct((T, SB_WIDTH), BF16),
        compiler_params=_params("arbitrary", "arbitrary", "arbitrary"),
        name="stickbreak_attn",
    )(qkv, qkv, qkv, g)


def _pool_kernel(x_ref, w_ref, sc_ref, o_ref, *, S):
    x = x_ref[...]
    pos = lax.broadcasted_iota(jnp.int32, (S, 1), 0)
    group = lax.broadcasted_iota(jnp.int32, (1, POOL_WIDTH), 1) // HEAD_DIM

    sums = []
    acc = x
    step = 1
    for _ in POOL_WINDOWS:
        acc = acc + jnp.where(pos >= step, pltpu.roll(acc, step, 0), 0.0)
        step *= 2
        sums.append(acc)
    sel = sums[-1]
    win = jnp.full((1, POOL_WIDTH), float(POOL_WINDOWS[-1]), F32)
    for gi in range(len(POOL_WINDOWS) - 2, -1, -1):
        sel = jnp.where(group == gi, sums[gi], sel)
        win = jnp.where(group == gi, float(POOL_WINDOWS[gi]), win)
    count = jnp.minimum((pos + 1).astype(F32), win)
    pooled = sel / count - x
    o_ref[...] = (_bdot(pooled, w_ref[...]) * sc_ref[...]).astype(o_ref.dtype)


def _pool(ppool, w_blk, scale, B, S):
    T = ppool.shape[0]
    return pl.pallas_call(
        functools.partial(_pool_kernel, S=S),
        grid=(B,),
        in_specs=[pl.BlockSpec((S, POOL_WIDTH), lambda b: (b, 0)),
                  pl.BlockSpec((POOL_WIDTH, POOL_WIDTH), lambda b: (0, 0)),
                  pl.BlockSpec((1, POOL_WIDTH), lambda b: (0, 0))],
        out_specs=pl.BlockSpec((S, POOL_WIDTH), lambda b: (b, 0)),
        out_shape=jax.ShapeDtypeStruct((T, POOL_WIDTH), BF16),
        compiler_params=_params("arbitrary"),
        name="multiscale_pool",
    )(ppool, w_blk, scale)


def _outproj_kernel(rw_ref, sb_ref, pool_ref, x_ref, gt_ref, sh_ref, sc_ref, w_ref, wr_ref,
                    xo_ref, h_ref, lg_ref):
    a = RW_WIDTH
    b = RW_WIDTH + SB_WIDTH
    mixed = (_dot(rw_ref[...], w_ref[0:a, :]) + _dot(sb_ref[...], w_ref[a:b, :])
             + _dot(pool_ref[...], w_ref[b:, :]))
    x = x_ref[...] + gt_ref[...] * mixed
    xo_ref[...] = x
    ms = jnp.mean(x * x, axis=-1, keepdims=True)
    h = x * lax.rsqrt(ms + RMS_EPS) * (1.0 + sc_ref[...]) + sh_ref[...]
    h_ref[...] = h.astype(BF16)
    lg_ref[...] = jnp.dot(h, wr_ref[...], precision=HIGHEST, preferred_element_type=F32)


def _outproj(rw, sb, pool, x2, mod5, l, w_bf, wr_pad, S):
    T, D = x2.shape
    tm = min(512, S)
    tpb = S // tm
    row = lambda n: pl.BlockSpec((tm, n), lambda i: (i, 0))
    return pl.pallas_call(
        _outproj_kernel,
        grid=(T // tm,),
        in_specs=[row(RW_WIDTH), row(SB_WIDTH), row(POOL_WIDTH), row(D),
                  _mod_spec(l, 2, tpb, D), _mod_spec(l, 3, tpb, D), _mod_spec(l, 4, tpb, D),
                  pl.BlockSpec((D, D), lambda i: (0, 0)),
                  pl.BlockSpec((D, LANES), lambda i: (0, 0))],
        out_specs=[row(D), row(D), row(LANES)],
        out_shape=[jax.ShapeDtypeStruct((T, D), F32), jax.ShapeDtypeStruct((T, D), BF16),
                   jax.ShapeDtypeStruct((T, LANES), F32)],
        compiler_params=_params("arbitrary"),
        name="out_proj",
    )(rw, sb, pool, x2, mod5, mod5, mod5, w_bf, wr_pad)


def _router_kernel(lg_ref, b_ref, u_ref, gate_ref, gatet_ref, rank_ref, *, n_exp, tm):
    per_group = n_exp // N_GROUPS
    neg = -jnp.inf
    lt = lg_ref[...].T[0:n_exp, :]
    scores = _sigmoid(lt)
    sel = scores + b_ref[...]
    sel3 = sel.reshape(N_GROUPS, per_group, tm)
    i_in = lax.broadcasted_iota(jnp.int32, sel3.shape, 1)
    m1 = jnp.max(sel3, axis=1, keepdims=True)
    first = jnp.min(jnp.where(sel3 == m1, i_in, per_group), axis=1, keepdims=True)
    m2 = jnp.max(jnp.where(i_in == first, neg, sel3), axis=1, keepdims=True)
    gs = m1 + m2
    gi = lax.broadcasted_iota(jnp.int32, gs.shape, 0)
    grank = jnp.zeros(gs.shape, jnp.int32)
    for g2 in range(N_GROUPS):
        o = gs[g2:g2 + 1]
        beats = jnp.where(o > gs, 1, jnp.where(o == gs, jnp.where(gi > g2, 1, 0), 0))
        grank = grank + beats
    masked = jnp.where(grank < TOPK_GROUPS, sel3, neg).reshape(n_exp, tm)
    ei = lax.broadcasted_iota(jnp.int32, masked.shape, 0)
    rank = jnp.zeros(masked.shape, jnp.int32)
    for e2 in range(n_exp):
        o = masked[e2:e2 + 1, :]
        beats = jnp.where(o > masked, 1, jnp.where(o == masked, jnp.where(ei > e2, 1, 0), 0))
        rank = rank + beats
    w = jnp.where(rank < TOP_K, scores, 0.0)
    denom = jnp.sum(w, axis=0, keepdims=True)
    gate = w / (denom + 1e-20) * ROUTED_SCALE
    gatet_ref[...] = gate
    routed = jnp.where(gate != 0.0, 1.0, 0.0).astype(BF16)
    rank_ref[...] = _dot(routed, u_ref[...]).astype(jnp.int32)
    gate = jnp.concatenate([gate, jnp.zeros((LANES - n_exp, tm), F32)], axis=0)
    gate_ref[...] = gate.T


def _router(logits, b_col, n_exp):
    T = logits.shape[0]
    tm = min(512, T)
    t_i = jnp.arange(tm)
    before = (t_i[:, None] < t_i[None, :]) & (t_i[:, None] // MOE_SUB == t_i[None, :] // MOE_SUB)
    return pl.pallas_call(
        functools.partial(_router_kernel, n_exp=n_exp, tm=tm),
        grid=(T // tm,),
        in_specs=[pl.BlockSpec((tm, LANES), lambda i: (i, 0)),
                  pl.BlockSpec((n_exp, 1), lambda i: (0, 0)),
                  pl.BlockSpec((tm, tm), lambda i: (0, 0))],
        out_specs=[pl.BlockSpec((tm, LANES), lambda i: (i, 0)),
                   pl.BlockSpec((n_exp, tm), lambda i: (0, i)),
                   pl.BlockSpec((n_exp, tm), lambda i: (0, i))],
        out_shape=[jax.ShapeDtypeStruct((T, LANES), F32),
                   jax.ShapeDtypeStruct((n_exp, T), F32),
                   jax.ShapeDtypeStruct((n_exp, T), jnp.int32)],
        compiler_params=_params("arbitrary"),
        name="router_topk",
    )(logits, b_col, before.astype(BF16))


def _experts_kernel(h_ref, gate_ref, gatet_ref, rank_ref, x_ref, gt_ref, wg_ref, wu_ref, wd_ref,
                    sg_ref, su_ref, sd_ref, gfin_ref, o_ref, xs_ref, ys_ref, pw_ref, flag_ref,
                    *, n_exp, final_norm, tm):
    e = pl.program_id(1)
    el = lax.rem(e, MOE_GROUP)
    n_sub = tm // MOE_SUB
    routed = e < n_exp

    @pl.when(e == 0)
    def _():
        o_ref[...] = jnp.zeros_like(o_ref)

    def swiglu(x, wg, wu):
        a = _dot(x, wg[...].astype(BF16))
        u = _dot(x, wu[...].astype(BF16))
        return a * _sigmoid(a) * u

    @pl.when(jnp.logical_and(routed, el == 0))
    def _group_start():
        g0 = pl.multiple_of(e, MOE_GROUP)
        gate_g = gatet_ref[pl.ds(g0, MOE_GROUP), :]
        rank_g = rank_ref[pl.ds(g0, MOE_GROUP), :]
        sel = gate_g != 0.0
        worst = jnp.float32(0.0)
        for s in range(n_sub):
            cols = slice(s * MOE_SUB, (s + 1) * MOE_SUB)
            cnt = jnp.sum(jnp.where(sel[:, cols], 1.0, 0.0), axis=1, keepdims=True)
            worst = jnp.maximum(worst, jnp.max(cnt))
        fits = worst <= float(MOE_CAP)
        flag_ref[0] = jnp.where(fits, 0, 1).astype(jnp.int32)

        @pl.when(fits)
        def _compact():
            slot = lax.broadcasted_iota(jnp.int32, (MOE_CAP, 1), 0)
            for s in range(n_sub):
                cols = slice(s * MOE_SUB, (s + 1) * MOE_SUB)
                ps, pws = [], []
                for j in range(MOE_GROUP):
                    hit = jnp.logical_and(sel[j:j + 1, cols], rank_g[j:j + 1, cols] == slot)
                    ps.append(jnp.where(hit, 1.0, 0.0))
                    pws.append(jnp.where(hit, gate_g[j:j + 1, cols], 0.0))
                p = jnp.concatenate(ps, axis=0).astype(BF16)
                pw_ref[s] = jnp.concatenate(pws, axis=0).astype(BF16)
                rows = _dot(p, h_ref[s * MOE_SUB:(s + 1) * MOE_SUB, :]).astype(BF16)
                for j in range(MOE_GROUP):
                    xs_ref[j, s * MOE_CAP:(s + 1) * MOE_CAP, :] = rows[j * MOE_CAP:(j + 1) * MOE_CAP]

    compact = flag_ref[0] == 0

    @pl.when(jnp.logical_and(routed, compact))
    def _():
        act = swiglu(xs_ref[el], wg_ref, wu_ref)
        y = _dot(act.astype(BF16), wd_ref[...].astype(BF16)).astype(BF16)
        for s in range(n_sub):
            ys_ref[s, el] = y[s * MOE_CAP:(s + 1) * MOE_CAP]

    @pl.when(jnp.logical_and(routed, jnp.logical_not(compact)))
    def _():
        lane = lax.broadcasted_iota(jnp.int32, (1, LANES), 1)
        gcol = jnp.sum(jnp.where(lane == e, gate_ref[...], 0.0), axis=-1, keepdims=True)
        act = swiglu(h_ref[...], wg_ref, wu_ref) * gcol
        o_ref[...] += _dot(act.astype(BF16), wd_ref[...].astype(BF16))

    @pl.when(jnp.logical_and(jnp.logical_and(routed, compact), el == MOE_GROUP - 1))
    def _scatter():
        for s in range(n_sub):
            y_s = jnp.concatenate([ys_ref[s, j] for j in range(MOE_GROUP)], axis=0)
            o_ref[s * MOE_SUB:(s + 1) * MOE_SUB, :] += lax.dot_general(
                pw_ref[s], y_s, (((0,), (0,)), ((), ())), preferred_element_type=F32)

    @pl.when(e == n_exp)
    def _():
        act = swiglu(h_ref[...], sg_ref, su_ref)
        x = x_ref[...] + gt_ref[...] * (o_ref[...] + _dot(act.astype(BF16), sd_ref[...].astype(BF16)))
        if final_norm:
            ms = jnp.mean(x * x, axis=-1, keepdims=True)
            x = x * lax.rsqrt(ms + RMS_EPS) * gfin_ref[...]
        o_ref[...] = x


def _experts(h2, gate, gate_t, rank_t, x2, mod5, l, w_gate, w_up, w_down, s_gate, s_up, s_down,
             g_final, S, final_norm):
    T, D = x2.shape
    n_exp, _, F = w_gate.shape[1:]
    assert n_exp % MOE_GROUP == 0
    tm = min(1024, S)
    tpb = S // tm
    n_sub = tm // MOE_SUB
    ex = lambda e: jnp.minimum(e, n_exp - 1)
    return pl.pallas_call(
        functools.partial(_experts_kernel, n_exp=n_exp, final_norm=final_norm, tm=tm),
        grid=(T // tm, n_exp + 1),
        in_specs=[pl.BlockSpec((tm, D), lambda i, e: (i, 0)),
                  pl.BlockSpec((tm, LANES), lambda i, e: (i, 0)),
                  pl.BlockSpec((n_exp, tm), lambda i, e: (0, i)),
                  pl.BlockSpec((n_exp, tm), lambda i, e: (0, i)),
                  pl.BlockSpec((tm, D), lambda i, e: (i, 0)),
                  _mod_spec(l, 5, tpb, D),
                  pl.BlockSpec((None, None, D, F), lambda i, e: (l, ex(e), 0, 0)),
                  pl.BlockSpec((None, None, D, F), lambda i, e: (l, ex(e), 0, 0)),
                  pl.BlockSpec((None, None, F, D), lambda i, e: (l, ex(e), 0, 0)),
                  pl.BlockSpec((None, D, F), lambda i, e: (l, 0, 0)),
                  pl.BlockSpec((None, D, F), lambda i, e: (l, 0, 0)),
                  pl.BlockSpec((None, F, D), lambda i, e: (l, 0, 0)),
                  pl.BlockSpec((1, D), lambda i, e: (0, 0))],
        out_specs=pl.BlockSpec((tm, D), lambda i, e: (i, 0)),
        out_shape=jax.ShapeDtypeStruct((T, D), F32),
        scratch_shapes=[pltpu.VMEM((MOE_GROUP, n_sub * MOE_CAP, D), BF16),
                        pltpu.VMEM((n_sub, MOE_GROUP, MOE_CAP, D), BF16),
                        pltpu.VMEM((n_sub, MOE_GROUP * MOE_CAP, MOE_SUB), BF16),
                        pltpu.SMEM((1,), jnp.int32)],
        compiler_params=_params("arbitrary", "arbitrary"),
        name="experts_ffn",
    )(h2, gate, gate_t, rank_t, x2, mod5, w_gate, w_up, w_down, s_gate, s_up, s_down, g_final)


def _pad_rows(w, lo, n_rows):
    return jnp.zeros((n_rows, w.shape[1]), w.dtype).at[lo:lo + w.shape[0]].set(w)


def kernel(x, c, w_ada, b_ada, w_in, w_vres, mu_rw, mu_vres, rw_w0, rw_w2, rw_a0, rw_a2, rw_v0, rw_v2, rw_g2, rw_k_k, rw_k_a, rw_r_k, rw_ln_g, rw_ln_b, sb_norm_g, pool_w, pool_scale, w_out, w_router, b_router, w_exp_gate, w_exp_up, w_exp_down, w_sh_gate, w_sh_up, w_sh_down, g_final):
    B, S, D = x.shape
    L = w_in.shape[0]
    T = B * S
    n_exp = w_router.shape[2]
    W = RW_WIDTH

    mod = _ada(c, w_ada, b_ada)
    mod5 = mod.reshape(L, B, 6, 1, D)

    x2 = x.reshape(T, D)
    v_first = None
    for l in range(L):
        has_vres = l > 0
        w_l = w_in[l]
        if has_vres:
            pad = jnp.zeros((D, LANES - RW_VRES_RANK), F32)
            w_l = jnp.concatenate([w_l, w_vres[l - 1], pad], axis=1)
        proj = _inproj(x2, mod5, l, w_l.astype(BF16), S, has_vres)
        prw, qkv, ppool = proj[0], proj[1], proj[2]
        pvres = proj[3] if has_vres else None

        o1 = 0
        o2 = RW_DECAY_RANK
        o3 = RW_DECAY_RANK + RW_ICLR_RANK
        rp = {
            "mu": mu_rw[l].reshape(1, RW_COLS),
            "w0": rw_w0[l].reshape(1, W),
            "w2": _pad_rows(rw_w2[l], o1, RW_LORA).astype(BF16),
            "a0": rw_a0[l].reshape(1, W),
            "a2": _pad_rows(rw_a2[l], o2, RW_LORA).astype(BF16),
            "g2": _pad_rows(rw_g2[l], o3, RW_LORA).astype(BF16),
            "k_k": rw_k_k[l].reshape(1, W),
            "k_a": rw_k_a[l].reshape(1, W),
            "r_k": rw_r_k[l].reshape(1, W),
            "ln_g": rw_ln_g[l].reshape(1, W),
            "ln_b": rw_ln_b[l].reshape(1, W),
        }
        if has_vres:
            rp["mu_v"] = jnp.concatenate(
                [mu_vres[l - 1], jnp.zeros((LANES - RW_VRES_RANK,), F32)]).reshape(1, LANES)
            rp["v0"] = rw_v0[l - 1].reshape(1, W)
            rp["v2"] = _pad_rows(rw_v2[l - 1], 0, LANES).astype(BF16)
        rw_out, v_first = _rwkv(prw, pvres, v_first, rp, B, S)

        sb_out = _sb_attention(qkv, sb_norm_g[l].reshape(1, SB_WIDTH), B, S)

        n_pool = len(POOL_WINDOWS)
        w_blk = (jnp.eye(n_pool, dtype=F32)[:, None, :, None] * pool_w[l][:, :, None, :]
                 ).reshape(POOL_WIDTH, POOL_WIDTH).astype(BF16)
        pool_out = _pool(ppool, w_blk, pool_scale[l].reshape(1, POOL_WIDTH), B, S)

        wr_pad = jnp.concatenate([w_router[l], jnp.zeros((D, LANES - n_exp), F32)], axis=1)
        x2, h2, logits = _outproj(rw_out, sb_out, pool_out, x2, mod5, l, w_out[l].astype(BF16),
                                  wr_pad, S)
        gate, gate_t, rank_t = _router(logits, b_router[l].reshape(n_exp, 1), n_exp)
        x2 = _experts(h2, gate, gate_t, rank_t, x2, mod5, l, w_exp_gate, w_exp_up, w_exp_down,
                      w_sh_gate, w_sh_up, w_sh_down, g_final.reshape(1, D), S,
                      final_norm=(l == L - 1))
    return x2.reshape(B, S, D)
```

```python
import functools

import jax
import jax.numpy as jnp
from jax import lax
from jax.experimental import pallas as pl
from jax.experimental.pallas import tpu as pltpu

F32 = jnp.float32
BF16 = jnp.bfloat16

HEAD_DIM = 64
RW_HEADS = 6
RW_WIDTH = RW_HEADS * HEAD_DIM
SB_HEADS = 6
SB_WIDTH = SB_HEADS * HEAD_DIM
POOL_WINDOWS = (2, 4, 8, 16)
POOL_WIDTH = len(POOL_WINDOWS) * HEAD_DIM
RW_DECAY_RANK = 32
RW_ICLR_RANK = 32
RW_VRES_RANK = 32
RW_GATE_RANK = 64
RW_LORA = RW_DECAY_RANK + RW_ICLR_RANK + RW_GATE_RANK
RW_COLS = 3 * RW_WIDTH + RW_LORA
SB_COLS = 3 * SB_WIDTH
IN_COLS = RW_COLS + SB_COLS + POOL_WIDTH
RW_GN_EPS = 64e-5
RMS_EPS = 1e-6
N_GROUPS = 8
TOPK_GROUPS = 4
TOP_K = 6
ROUTED_SCALE = 2.5

LANES = 128
RW_CHUNK = 64
RW_SUB = 16
VMEM_LIMIT = 48 * 1024 * 1024
MOE_GROUP = 8
MOE_SUB = 256
MOE_CAP = 48
SB_DEAD_LOG = -104.0

HIGHEST = lax.Precision.HIGHEST


def _dot(a, b):
    return jnp.dot(a, b, preferred_element_type=F32)


def _bdot(a, b):
    return jnp.dot(a.astype(BF16), b.astype(BF16), preferred_element_type=F32)


def _bdot_nt(a, b):
    return lax.dot_general(a.astype(BF16), b.astype(BF16), (((1,), (1,)), ((), ())),
                           preferred_element_type=F32)


def _bdot_tn(a, b):
    return lax.dot_general(a.astype(BF16), b.astype(BF16), (((0,), (0,)), ((), ())),
                           preferred_element_type=F32)


def _split_dot(x, w, terms, left=False):
    acc = None
    rem = x
    for t in range(terms):
        part = rem.astype(BF16)
        d = _dot(w, part) if left else _dot(part, w)
        acc = d if acc is None else acc + d
        if t + 1 < terms:
            rem = rem - part.astype(F32)
    return acc


def _sigmoid(x):
    return 1.0 / (1.0 + jnp.exp(-x))


def _softplus(x):
    return jnp.maximum(x, 0.0) + jnp.log(1.0 + jnp.exp(-jnp.abs(x)))


def _params(*sem):
    return pltpu.CompilerParams(dimension_semantics=sem, vmem_limit_bytes=VMEM_LIMIT)


def _ada_kernel(c_ref, w_ref, b_ref, o_ref):
    c = c_ref[...]
    cond = c * _sigmoid(c)
    o_ref[...] = jnp.dot(cond, w_ref[...], precision=HIGHEST,
                         preferred_element_type=F32) + b_ref[...]


def _ada(c, w_ada, b_ada):
    L, D, N = w_ada.shape
    B = c.shape[0]
    tn = 1536 if N % 1536 == 0 else N
    return pl.pallas_call(
        _ada_kernel,
        grid=(L, N // tn),
        in_specs=[pl.BlockSpec((B, D), lambda l, n: (0, 0)),
                  pl.BlockSpec((None, D, tn), lambda l, n: (l, 0, n)),
                  pl.BlockSpec((None, 1, tn), lambda l, n: (l, 0, n))],
        out_specs=pl.BlockSpec((None, B, tn), lambda l, n: (l, 0, n)),
        out_shape=jax.ShapeDtypeStruct((L, B, N), F32),
        compiler_params=_params("arbitrary", "arbitrary"),
        name="ada_mod",
    )(c, w_ada, b_ada.reshape(L, 1, N))


def _mod_spec(l, j, tiles_per_batch, D):
    return pl.BlockSpec((None, None, None, 1, D),
                        lambda i, *_: (l, i // tiles_per_batch, j, 0, 0))


def _inproj_kernel(x_ref, sh_ref, sc_ref, w_ref, prw_ref, qkv_ref, pool_ref, *vres_ref):
    x = x_ref[...]
    ms = jnp.mean(x * x, axis=-1, keepdims=True)
    h = x * lax.rsqrt(ms + RMS_EPS) * (1.0 + sc_ref[...]) + sh_ref[...]
    hb = h.astype(BF16)
    prw_ref[...] = _dot(hb, w_ref[:, 0:RW_COLS])
    qkv_ref[...] = _dot(hb, w_ref[:, RW_COLS:RW_COLS + SB_COLS]).astype(BF16)
    pool_ref[...] = _dot(hb, w_ref[:, RW_COLS + SB_COLS:IN_COLS])
    if vres_ref:
        vres_ref[0][...] = _dot(hb, w_ref[:, IN_COLS:IN_COLS + LANES])


def _inproj(x2, mod5, l, w_bf, S, has_vres):
    T, D = x2.shape
    tm = min(512, S)
    tpb = S // tm
    n_w = w_bf.shape[1]
    out_shape = [jax.ShapeDtypeStruct((T, RW_COLS), F32),
                 jax.ShapeDtypeStruct((T, SB_COLS), BF16),
                 jax.ShapeDtypeStruct((T, POOL_WIDTH), F32)]
    out_specs = [pl.BlockSpec((tm, RW_COLS), lambda i: (i, 0)),
                 pl.BlockSpec((tm, SB_COLS), lambda i: (i, 0)),
                 pl.BlockSpec((tm, POOL_WIDTH), lambda i: (i, 0))]
    if has_vres:
        out_shape.append(jax.ShapeDtypeStruct((T, LANES), F32))
        out_specs.append(pl.BlockSpec((tm, LANES), lambda i: (i, 0)))
    return pl.pallas_call(
        _inproj_kernel,
        grid=(T // tm,),
        in_specs=[pl.BlockSpec((tm, D), lambda i: (i, 0)),
                  _mod_spec(l, 0, tpb, D), _mod_spec(l, 1, tpb, D),
                  pl.BlockSpec((D, n_w), lambda i: (0, 0))],
        out_specs=out_specs,
        out_shape=out_shape,
        compiler_params=_params("arbitrary"),
        name="in_proj",
    )(x2, mod5, mod5, w_bf)


def _unit_lower_solve(a_strict, rhs, eye, sub_mask):
    n_p = range(len(a_strict))
    ad = [jnp.where(sub_mask, a, 0.0) for a in a_strict]
    ao = [a_strict[i] - ad[i] for i in n_p]
    inv_d = [eye + ad[i] for i in n_p]
    pw = ad
    n = 2
    while n < RW_SUB:
        pw = [_bdot(pw[i], pw[i]) for i in n_p]
        inv_d = [inv_d[i] + _bdot(inv_d[i], pw[i]) for i in n_p]
        n *= 2
    m = [_bdot(inv_d[i], ao[i]) for i in n_p]
    t = [_bdot(inv_d[i], rhs[i]) for i in n_p]
    terms = []
    n = 1
    pw = m
    while n < RW_CHUNK // RW_SUB:
        terms.append(pw)
        n *= 2
        if n < RW_CHUNK // RW_SUB:
            pw = [_bdot(pw[i], pw[i]) for i in n_p]
    for pw in reversed(terms):
        t = [t[i] + _bdot(pw[i], t[i]) for i in n_p]
    return t


def _rwkv_kernel(*refs, has_vres, tm):
    if has_vres:
        (prw_ref, pvr_ref, vfirst_ref, mu_ref, w0_ref, w2_ref, a0_ref, a2_ref, g2_ref,
         kk_ref, ka_ref, rk_ref, lng_ref, lnb_ref, hblk_ref, ltri_ref,
         muv_ref, v0_ref, v2_ref,
         out_ref,
         carry_ref, carryv_ref, state_ref, at_ref, rt_ref, bt_ref, kt_ref, v_ref, cum_ref,
         y_ref, g_ref, bonus_ref) = refs
        vout_ref = None
    else:
        (prw_ref, mu_ref, w0_ref, w2_ref, a0_ref, a2_ref, g2_ref,
         kk_ref, ka_ref, rk_ref, lng_ref, lnb_ref, hblk_ref, ltri_ref,
         out_ref, vout_ref,
         carry_ref, state_ref, at_ref, rt_ref, bt_ref, kt_ref, v_ref, cum_ref,
         y_ref, g_ref, bonus_ref) = refs
        pvr_ref = vfirst_ref = muv_ref = v0_ref = v2_ref = carryv_ref = None

    s = pl.program_id(1)

    @pl.when(s == 0)
    def _():
        carry_ref[...] = jnp.zeros_like(carry_ref)
        state_ref[...] = jnp.zeros_like(state_ref)
        if has_vres:
            carryv_ref[...] = jnp.zeros_like(carryv_ref)

    row0 = lax.broadcasted_iota(jnp.int32, (tm, 1), 0) == 0

    def shift(p, c_ref, mu):
        prev = jnp.where(row0, c_ref[0:1, :], pltpu.roll(p, 1, 0))
        c_ref[0:1, :] = p[tm - 1:tm, :]
        return p + (prev - p) * mu

    W = RW_WIDTH
    f = shift(prw_ref[...], carry_ref, mu_ref[...])
    r = f[:, 0:W]
    k = f[:, W:2 * W]
    v = f[:, 2 * W:3 * W]
    lora = f[:, 3 * W:3 * W + RW_LORA]

    wlin = w0_ref[...] + _bdot(jnp.tanh(lora), w2_ref[...])
    lw = -jnp.exp(-_softplus(-wlin) - 0.5)
    a = _sigmoid(a0_ref[...] + _bdot(lora, a2_ref[...]))
    g_ref[...] = _bdot(_sigmoid(lora), g2_ref[...])
    if has_vres:
        fv = shift(pvr_ref[...], carryv_ref, muv_ref[...])
        v_gate = _sigmoid(v0_ref[...] + _bdot(fv, v2_ref[...]))
        v = v + (vfirst_ref[...] - v) * v_gate
    else:
        vout_ref[...] = v

    hblk = hblk_ref[...]
    kk = k * kk_ref[...]
    kk = kk / jnp.maximum(jnp.sqrt(_split_dot(kk * kk, hblk, 2)), 1e-12)
    k = k * (1.0 + (a - 1.0) * ka_ref[...])
    bonus_ref[...] = _split_dot(r * k * rk_ref[...], hblk, 2) * v

    cum = _split_dot(lw, ltri_ref[...], 3, left=True)
    p_inc = jnp.exp(cum)
    inv = jnp.exp(-cum)
    at_ref[...] = -kk * jnp.exp(cum - lw)
    rt_ref[...] = r * p_inc
    bt_ref[...] = kk * a * inv
    kt_ref[...] = k * inv
    v_ref[...] = v
    cum_ref[...] = cum

    C = RW_CHUNK
    ri = lax.broadcasted_iota(jnp.int32, (C, C), 0)
    ci = lax.broadcasted_iota(jnp.int32, (C, C), 1)
    strict = ri > ci
    incl = ri >= ci
    eye = (ri == ci).astype(F32)
    sub_mask = (ri // RW_SUB) == (ci // RW_SUB)

    def chunk(c, carry):
        r0 = pl.multiple_of(c * C, C)
        rows = pl.ds(r0, C)
        tail = pl.ds(pl.multiple_of(r0 + C - 8, 8), 8)
        hs = range(RW_HEADS)
        ls = [slice(h * HEAD_DIM, (h + 1) * HEAD_DIM) for h in hs]
        at = [at_ref[rows, ls[h]] for h in hs]
        rt = [rt_ref[rows, ls[h]] for h in hs]
        bt = [bt_ref[rows, ls[h]] for h in hs]
        kt = [kt_ref[rows, ls[h]] for h in hs]
        vv = [v_ref[rows, ls[h]] for h in hs]
        pc = [jnp.exp(cum_ref[tail, ls[h]][7:8, :]) for h in hs]
        st = [state_ref[h] for h in hs]
        a_ab = [jnp.where(strict, _bdot_nt(at[h], bt[h]), 0.0) for h in hs]
        a_ak = [jnp.where(strict, _bdot_nt(at[h], kt[h]), 0.0) for h in hs]
        a_rb = [jnp.where(incl, _bdot_nt(rt[h], bt[h]), 0.0) for h in hs]
        a_rk = [jnp.where(incl, _bdot_nt(rt[h], kt[h]), 0.0) for h in hs]
        a_s = [_bdot_nt(at[h], st[h]) for h in hs]
        r_s = [_bdot_nt(rt[h], st[h]) for h in hs]
        rhs = [a_s[h] + _bdot(a_ak[h], vv[h]) for h in hs]
        y0 = [r_s[h] + _bdot(a_rk[h], vv[h]) for h in hs]
        s0 = [st[h] * pc[h] + _bdot_tn(vv[h], kt[h] * pc[h]) for h in hs]
        z = _unit_lower_solve(a_ab, rhs, eye, sub_mask)
        y = [y0[h] + _bdot(a_rb[h], z[h]) for h in hs]
        s1 = [s0[h] + _bdot_tn(z[h], bt[h] * pc[h]) for h in hs]
        for h in hs:
            y_ref[rows, ls[h]] = y[h]
            state_ref[h] = s1[h]
        return carry

    lax.fori_loop(0, tm // C, chunk, 0)

    y = y_ref[...]
    inv_n = 1.0 / HEAD_DIM
    mean = _split_dot(y, hblk, 2) * inv_n
    d = y - mean
    var = _split_dot(d * d, hblk, 2) * inv_n
    yn = d * lax.rsqrt(var + RW_GN_EPS) * lng_ref[...] + lnb_ref[...]
    out_ref[...] = ((yn + bonus_ref[...]) * g_ref[...]).astype(out_ref.dtype)


def _rwkv(prw, pvres, v_first, p, B, S):
    T = prw.shape[0]
    tm = min(512, S)
    ns = S // tm
    has_vres = pvres is not None
    W = RW_WIDTH
    row = lambda b, s: (b * ns + s, 0)
    const = lambda b, s: (0, 0)
    vec = lambda n: pl.BlockSpec((1, n), const)
    mat = lambda m, n: pl.BlockSpec((m, n), const)

    hblk = jnp.kron(jnp.eye(RW_HEADS, dtype=F32), jnp.ones((HEAD_DIM, HEAD_DIM), F32)).astype(BF16)
    ltri = jnp.kron(jnp.eye(tm // RW_CHUNK, dtype=F32),
                    jnp.tril(jnp.ones((RW_CHUNK, RW_CHUNK), F32))).astype(BF16)

    inputs = [prw]
    in_specs = [pl.BlockSpec((tm, RW_COLS), row)]
    if has_vres:
        inputs += [pvres, v_first]
        in_specs += [pl.BlockSpec((tm, LANES), row), pl.BlockSpec((tm, W), row)]
    inputs += [p["mu"], p["w0"], p["w2"], p["a0"], p["a2"], p["g2"], p["k_k"], p["k_a"],
               p["r_k"], p["ln_g"], p["ln_b"], hblk, ltri]
    in_specs += [vec(RW_COLS), vec(W), mat(RW_LORA, W), vec(W), mat(RW_LORA, W), mat(RW_LORA, W),
                 vec(W), vec(W), vec(W), vec(W), vec(W), mat(W, W), mat(tm, tm)]
    if has_vres:
        inputs += [p["mu_v"], p["v0"], p["v2"]]
        in_specs += [vec(LANES), vec(W), mat(LANES, W)]

    out_shape = [jax.ShapeDtypeStruct((T, W), BF16)]
    out_specs = [pl.BlockSpec((tm, W), row)]
    if not has_vres:
        out_shape.append(jax.ShapeDtypeStruct((T, W), F32))
        out_specs.append(pl.BlockSpec((tm, W), row))

    big = lambda: pltpu.VMEM((tm, W), F32)
    scratch = [pltpu.VMEM((8, RW_COLS), F32)]
    if has_vres:
        scratch.append(pltpu.VMEM((8, LANES), F32))
    scratch += [pltpu.VMEM((RW_HEADS, HEAD_DIM, HEAD_DIM), F32)] + [big() for _ in range(9)]

    res = pl.pallas_call(
        functools.partial(_rwkv_kernel, has_vres=has_vres, tm=tm),
        grid=(B, ns),
        in_specs=in_specs,
        out_specs=out_specs,
        out_shape=out_shape,
        scratch_shapes=scratch,
        compiler_params=_params("arbitrary", "arbitrary"),
        name="rwkv7_mix",
    )(*inputs)
    return (res[0], v_first) if has_vres else (res[0], res[1])


def _sb_kernel(q_ref, k_ref, v_ref, g_ref, o_ref, *, tq):
    i = pl.program_id(2)
    lane = lax.broadcasted_iota(jnp.int32, (1, LANES), 1)
    head_of_lane = lane // HEAD_DIM
    ri = lax.broadcasted_iota(jnp.int32, (tq, tq), 0)
    ci = lax.broadcasted_iota(jnp.int32, (tq, tq), 1)
    causal = ri > ci
    u_incl = (ri >= ci).astype(BF16)
    scale = HEAD_DIM ** -0.5
    q_all = q_ref[...]
    heads = range(2)
    qh = [jnp.where(head_of_lane == h, q_all, jnp.zeros_like(q_all)) * jnp.asarray(scale, BF16)
          for h in heads]

    def tile(j, acc, run, diagonal):
        k0 = pl.multiple_of(j * tq, tq)
        kt = k_ref[pl.ds(k0, tq), :]
        vt = v_ref[pl.ds(k0, tq), :]
        z = [lax.dot_general(qh[h], kt, (((1,), (1,)), ((), ())), preferred_element_type=F32)
             for h in heads]
        ln = [-_softplus(z[h]) for h in heads]
        lnm = [jnp.where(causal, ln[h], 0.0) for h in heads] if diagonal else ln
        inc = [_split_dot(lnm[h], u_incl, 2) for h in heads]
        att = [jnp.exp(z[h] + ln[h] + (inc[h] - lnm[h]) + run[h]) for h in heads]
        if diagonal:
            att = [jnp.where(causal, att[h], 0.0) for h in heads]
        acc = [acc[h] + _dot(att[h].astype(BF16), vt) for h in heads]
        run = [run[h] + inc[h][:, 0:1] for h in heads]
        return acc, run

    acc, run = tile(i, [jnp.zeros((tq, LANES), F32)] * 2, [jnp.zeros((tq, 1), F32)] * 2, True)

    def alive(run):
        return jnp.max(jnp.maximum(run[0], run[1]))

    def cond(carry):
        j, _, _, top = carry
        return jnp.logical_and(j >= 0, top > SB_DEAD_LOG)

    def body(carry):
        j, acc, run, _ = carry
        acc, run = tile(j, acc, run, False)
        return j - 1, acc, run, alive(run)

    _, accs, _, _ = lax.while_loop(cond, body, (i - 1, acc, run, alive(run)))

    o = jnp.where(head_of_lane == 0, accs[0], accs[1])
    o2 = o * o
    s0 = jnp.sum(jnp.where(head_of_lane == 0, o2, 0.0), axis=-1, keepdims=True)
    s1 = jnp.sum(o2, axis=-1, keepdims=True) - s0
    ms = jnp.where(head_of_lane == 0, s0, s1) * (1.0 / HEAD_DIM)
    o_ref[...] = (o * lax.rsqrt(ms + RMS_EPS) * g_ref[...]).astype(o_ref.dtype)


def _sb_attention(qkv, g, B, S):
    T = qkv.shape[0]
    tq = min(256, S)
    nq = S // tq
    npair = SB_WIDTH // LANES
    return pl.pallas_call(
        functools.partial(_sb_kernel, tq=tq),
        grid=(B, npair, nq),
        in_specs=[pl.BlockSpec((tq, LANES), lambda b, p, i: (b * nq + i, p)),
                  pl.BlockSpec((S, LANES), lambda b, p, i: (b, npair + p)),
                  pl.BlockSpec((S, LANES), lambda b, p, i: (b, 2 * npair + p)),
                  pl.BlockSpec((1, LANES), lambda b, p, i: (0, p))],
        out_specs=pl.BlockSpec((tq, LANES), lambda b, p, i: (b * nq + i, p)),
        out_shape=jax.ShapeDtypeStruct((T, SB_WIDTH), BF16),
        compiler_params=_params("arbitrary", "arbitrary", "arbitrary"),
        name="stickbreak_attn",
    )(qkv, qkv, qkv, g)


def _pool_kernel(x_ref, w_ref, sc_ref, o_ref, *, S):
    x = x_ref[...]
    pos = lax.broadcasted_iota(jnp.int32, (S, 1), 0)
    group = lax.broadcasted_iota(jnp.int32, (1, POOL_WIDTH), 1) // HEAD_DIM

    sums = []
    acc = x
    step = 1
    for _ in POOL_WINDOWS:
        acc = acc + jnp.where(pos >= step, pltpu.roll(acc, step, 0), 0.0)
        step *= 2
        sums.append(acc)
    sel = sums[-1]
    win = jnp.full((1, POOL_WIDTH), float(POOL_WINDOWS[-1]), F32)
    for gi in range(len(POOL_WINDOWS) - 2, -1, -1):
        sel = jnp.where(group == gi, sums[gi], sel)
        win = jnp.where(group == gi, float(POOL_WINDOWS[gi]), win)
    count = jnp.minimum((pos + 1).astype(F32), win)
    pooled = sel / count - x
    o_ref[...] = (_bdot(pooled, w_ref[...]) * sc_ref[...]).astype(o_ref.dtype)


def _pool(ppool, w_blk, scale, B, S):
    T = ppool.shape[0]
    return pl.pallas_call(
        functools.partial(_pool_kernel, S=S),
        grid=(B,),
        in_specs=[pl.BlockSpec((S, POOL_WIDTH), lambda b: (b, 0)),
                  pl.BlockSpec((POOL_WIDTH, POOL_WIDTH), lambda b: (0, 0)),
                  pl.BlockSpec((1, POOL_WIDTH), lambda b: (0, 0))],
        out_specs=pl.BlockSpec((S, POOL_WIDTH), lambda b: (b, 0)),
        out_shape=jax.ShapeDtypeStruct((T, POOL_WIDTH), BF16),
        compiler_params=_params("arbitrary"),
        name="multiscale_pool",
    )(ppool, w_blk, scale)


def _outproj_kernel(rw_ref, sb_ref, pool_ref, x_ref, gt_ref, sh_ref, sc_ref, w_ref, wr_ref,
                    xo_ref, h_ref, lg_ref):
    a = RW_WIDTH
    b = RW_WIDTH + SB_WIDTH
    mixed = (_dot(rw_ref[...], w_ref[0:a, :]) + _dot(sb_ref[...], w_ref[a:b, :])
             + _dot(pool_ref[...], w_ref[b:, :]))
    x = x_ref[...] + gt_ref[...] * mixed
    xo_ref[...] = x
    ms = jnp.mean(x * x, axis=-1, keepdims=True)
    h = x * lax.rsqrt(ms + RMS_EPS) * (1.0 + sc_ref[...]) + sh_ref[...]
    h_ref[...] = h.astype(BF16)
    lg_ref[...] = jnp.dot(h, wr_ref[...], precision=HIGHEST, preferred_element_type=F32)


def _outproj(rw, sb, pool, x2, mod5, l, w_bf, wr_pad, S):
    T, D = x2.shape
    tm = min(512, S)
    tpb = S // tm
    row = lambda n: pl.BlockSpec((tm, n), lambda i: (i, 0))
    return pl.pallas_call(
        _outproj_kernel,
        grid=(T // tm,),
        in_specs=[row(RW_WIDTH), row(SB_WIDTH), row(POOL_WIDTH), row(D),
                  _mod_spec(l, 2, tpb, D), _mod_spec(l, 3, tpb, D), _mod_spec(l, 4, tpb, D),
                  pl.BlockSpec((D, D), lambda i: (0, 0)),
                  pl.BlockSpec((D, LANES), lambda i: (0, 0))],
        out_specs=[row(D), row(D), row(LANES)],
        out_shape=[jax.ShapeDtypeStruct((T, D), F32), jax.ShapeDtypeStruct((T, D), BF16),
                   jax.ShapeDtypeStruct((T, LANES), F32)],
        compiler_params=_params("arbitrary"),
        name="out_proj",
    )(rw, sb, pool, x2, mod5, mod5, mod5, w_bf, wr_pad)


def _router_kernel(lg_ref, b_ref, u_ref, gate_ref, gatet_ref, rank_ref, *, n_exp, tm):
    per_group = n_exp // N_GROUPS
    neg = -jnp.inf
    lt = lg_ref[...].T[0:n_exp, :]
    scores = _sigmoid(lt)
    sel = scores + b_ref[...]
    sel3 = sel.reshape(N_GROUPS, per_group, tm)
    i_in = lax.broadcasted_iota(jnp.int32, sel3.shape, 1)
    m1 = jnp.max(sel3, axis=1, keepdims=True)
    first = jnp.min(jnp.where(sel3 == m1, i_in, per_group), axis=1, keepdims=True)
    m2 = jnp.max(jnp.where(i_in == first, neg, sel3), axis=1, keepdims=True)
    gs = m1 + m2
    gi = lax.broadcasted_iota(jnp.int32, gs.shape, 0)
    grank = jnp.zeros(gs.shape, jnp.int32)
    for g2 in range(N_GROUPS):
        o = gs[g2:g2 + 1]
        beats = jnp.where(o > gs, 1, jnp.where(o == gs, jnp.where(gi > g2, 1, 0), 0))
        grank = grank + beats
    masked = jnp.where(grank < TOPK_GROUPS, sel3, neg).reshape(n_exp, tm)
    ei = lax.broadcasted_iota(jnp.int32, masked.shape, 0)
    rank = jnp.zeros(masked.shape, jnp.int32)
    for e2 in range(n_exp):
        o = masked[e2:e2 + 1, :]
        beats = jnp.where(o > masked, 1, jnp.where(o == masked, jnp.where(ei > e2, 1, 0), 0))
        rank = rank + beats
    w = jnp.where(rank < TOP_K, scores, 0.0)
    denom = jnp.sum(w, axis=0, keepdims=True)
    gate = w / (denom + 1e-20) * ROUTED_SCALE
    gatet_ref[...] = gate
    routed = jnp.where(gate != 0.0, 1.0, 0.0).astype(BF16)
    rank_ref[...] = _dot(routed, u_ref[...]).astype(jnp.int32)
    gate = jnp.concatenate([gate, jnp.zeros((LANES - n_exp, tm), F32)], axis=0)
    gate_ref[...] = gate.T


def _router(logits, b_col, n_exp):
    T = logits.shape[0]
    tm = min(512, T)
    t_i = jnp.arange(tm)
    before = (t_i[:, None] < t_i[None, :]) & (t_i[:, None] // MOE_SUB == t_i[None, :] // MOE_SUB)
    return pl.pallas_call(
        functools.partial(_router_kernel, n_exp=n_exp, tm=tm),
        grid=(T // tm,),
        in_specs=[pl.BlockSpec((tm, LANES), lambda i: (i, 0)),
                  pl.BlockSpec((n_exp, 1), lambda i: (0, 0)),
                  pl.BlockSpec((tm, tm), lambda i: (0, 0))],
        out_specs=[pl.BlockSpec((tm, LANES), lambda i: (i, 0)),
                   pl.BlockSpec((n_exp, tm), lambda i: (0, i)),
                   pl.BlockSpec((n_exp, tm), lambda i: (0, i))],
        out_shape=[jax.ShapeDtypeStruct((T, LANES), F32),
                   jax.ShapeDtypeStruct((n_exp, T), F32),
                   jax.ShapeDtypeStruct((n_exp, T), jnp.int32)],
        compiler_params=_params("arbitrary"),
        name="router_topk",
    )(logits, b_col, before.astype(BF16))


def _experts_kernel(h_ref, gate_ref, gatet_ref, rank_ref, x_ref, gt_ref, wg_ref, wu_ref, wd_ref,
                    sg_ref, su_ref, sd_ref, gfin_ref, o_ref, xs_ref, ys_ref, pw_ref,
                    *, n_exp, final_norm, tm):
    e = pl.program_id(1)
    el = lax.rem(e, MOE_GROUP)
    n_sub = tm // MOE_SUB
    routed = e < n_exp

    @pl.when(e == 0)
    def _():
        o_ref[...] = jnp.zeros_like(o_ref)

    def swiglu(x, wg, wu):
        a = _dot(x, wg[...].astype(BF16))
        u = _dot(x, wu[...].astype(BF16))
        return a * _sigmoid(a) * u

    g0 = pl.multiple_of(jnp.minimum(e, n_exp - 1) // MOE_GROUP * MOE_GROUP, MOE_GROUP)
    gate_g = gatet_ref[pl.ds(g0, MOE_GROUP), :]
    sel = gate_g != 0.0
    worst = jnp.zeros((MOE_GROUP, 1), F32)
    for s in range(n_sub):
        cols = slice(s * MOE_SUB, (s + 1) * MOE_SUB)
        worst = jnp.maximum(worst, jnp.sum(jnp.where(sel[:, cols], 1.0, 0.0), axis=1, keepdims=True))
    fits = worst <= float(MOE_CAP)
    row = lax.broadcasted_iota(jnp.int32, (MOE_GROUP, 1), 0)
    compact = jnp.max(jnp.where(jnp.logical_and(row == el, fits), 1.0, 0.0)) > 0.5

    @pl.when(jnp.logical_and(routed, el == 0))
    def _group_start():
        rank_g = rank_ref[pl.ds(g0, MOE_GROUP), :]
        sel_fit = jnp.logical_and(sel, fits)
        slot = lax.broadcasted_iota(jnp.int32, (MOE_CAP, 1), 0)
        for s in range(n_sub):
            cols = slice(s * MOE_SUB, (s + 1) * MOE_SUB)
            ps, pws = [], []
            for j in range(MOE_GROUP):
                hit = jnp.logical_and(sel_fit[j:j + 1, cols], rank_g[j:j + 1, cols] == slot)
                ps.append(jnp.where(hit, 1.0, 0.0))
                pws.append(jnp.where(hit, gate_g[j:j + 1, cols], 0.0))
            p = jnp.concatenate(ps, axis=0).astype(BF16)
            pw_ref[s] = jnp.concatenate(pws, axis=0).astype(BF16)
            rows = _dot(p, h_ref[s * MOE_SUB:(s + 1) * MOE_SUB, :]).astype(BF16)
            for j in range(MOE_GROUP):
                xs_ref[j, s * MOE_CAP:(s + 1) * MOE_CAP, :] = rows[j * MOE_CAP:(j + 1) * MOE_CAP]

    @pl.when(jnp.logical_and(routed, compact))
    def _():
        act = swiglu(xs_ref[el], wg_ref, wu_ref)
        y = _dot(act.astype(BF16), wd_ref[...].astype(BF16)).astype(BF16)
        for s in range(n_sub):
            ys_ref[s, el] = y[s * MOE_CAP:(s + 1) * MOE_CAP]

    @pl.when(jnp.logical_and(routed, jnp.logical_not(compact)))
    def _():
        lane = lax.broadcasted_iota(jnp.int32, (1, LANES), 1)
        gcol = jnp.sum(jnp.where(lane == e, gate_ref[...], 0.0), axis=-1, keepdims=True)
        act = swiglu(h_ref[...], wg_ref, wu_ref) * gcol
        o_ref[...] += _dot(act.astype(BF16), wd_ref[...].astype(BF16))
        for s in range(n_sub):
            ys_ref[s, el] = jnp.zeros((MOE_CAP, o_ref.shape[1]), BF16)

    @pl.when(jnp.logical_and(routed, el == MOE_GROUP - 1))
    def _scatter():
        for s in range(n_sub):
            y_s = jnp.concatenate([ys_ref[s, j] for j in range(MOE_GROUP)], axis=0)
            o_ref[s * MOE_SUB:(s + 1) * MOE_SUB, :] += lax.dot_general(
                pw_ref[s], y_s, (((0,), (0,)), ((), ())), preferred_element_type=F32)

    @pl.when(e == n_exp)
    def _():
        act = swiglu(h_ref[...], sg_ref, su_ref)
        x = x_ref[...] + gt_ref[...] * (o_ref[...] + _dot(act.astype(BF16), sd_ref[...].astype(BF16)))
        if final_norm:
            ms = jnp.mean(x * x, axis=-1, keepdims=True)
            x = x * lax.rsqrt(ms + RMS_EPS) * gfin_ref[...]
        o_ref[...] = x


def _experts(h2, gate, gate_t, rank_t, x2, mod5, l, w_gate, w_up, w_down, s_gate, s_up, s_down,
             g_final, S, final_norm):
    T, D = x2.shape
    n_exp, _, F = w_gate.shape[1:]
    assert n_exp % MOE_GROUP == 0
    tm = min(1024, S)
    tpb = S // tm
    n_sub = tm // MOE_SUB
    ex = lambda e: jnp.minimum(e, n_exp - 1)
    return pl.pallas_call(
        functools.partial(_experts_kernel, n_exp=n_exp, final_norm=final_norm, tm=tm),
        grid=(T // tm, n_exp + 1),
        in_specs=[pl.BlockSpec((tm, D), lambda i, e: (i, 0)),
                  pl.BlockSpec((tm, LANES), lambda i, e: (i, 0)),
                  pl.BlockSpec((n_exp, tm), lambda i, e: (0, i)),
                  pl.BlockSpec((n_exp, tm), lambda i, e: (0, i)),
                  pl.BlockSpec((tm, D), lambda i, e: (i, 0)),
                  _mod_spec(l, 5, tpb, D),
                  pl.BlockSpec((None, None, D, F), lambda i, e: (l, ex(e), 0, 0)),
                  pl.BlockSpec((None, None, D, F), lambda i, e: (l, ex(e), 0, 0)),
                  pl.BlockSpec((None, None, F, D), lambda i, e: (l, ex(e), 0, 0)),
                  pl.BlockSpec((None, D, F), lambda i, e: (l, 0, 0)),
                  pl.BlockSpec((None, D, F), lambda i, e: (l, 0, 0)),
                  pl.BlockSpec((None, F, D), lambda i, e: (l, 0, 0)),
                  pl.BlockSpec((1, D), lambda i, e: (0, 0))],
        out_specs=pl.BlockSpec((tm, D), lambda i, e: (i, 0)),
        out_shape=jax.ShapeDtypeStruct((T, D), F32),
        scratch_shapes=[pltpu.VMEM((MOE_GROUP, n_sub * MOE_CAP, D), BF16),
                        pltpu.VMEM((n_sub, MOE_GROUP, MOE_CAP, D), BF16),
                        pltpu.VMEM((n_sub, MOE_GROUP * MOE_CAP, MOE_SUB), BF16)],
        compiler_params=_params("arbitrary", "arbitrary"),
        name="experts_ffn",
    )(h2, gate, gate_t, rank_t, x2, mod5, w_gate, w_up, w_down, s_gate, s_up, s_down, g_final)


def _pad_rows(w, lo, n_rows):
    return jnp.zeros((n_rows, w.shape[1]), w.dtype).at[lo:lo + w.shape[0]].set(w)


def kernel(x, c, w_ada, b_ada, w_in, w_vres, mu_rw, mu_vres, rw_w0, rw_w2, rw_a0, rw_a2, rw_v0, rw_v2, rw_g2, rw_k_k, rw_k_a, rw_r_k, rw_ln_g, rw_ln_b, sb_norm_g, pool_w, pool_scale, w_out, w_router, b_router, w_exp_gate, w_exp_up, w_exp_down, w_sh_gate, w_sh_up, w_sh_down, g_final):
    B, S, D = x.shape
    L = w_in.shape[0]
    T = B * S
    n_exp = w_router.shape[2]
    W = RW_WIDTH

    mod = _ada(c, w_ada, b_ada)
    mod5 = mod.reshape(L, B, 6, 1, D)

    x2 = x.reshape(T, D)
    v_first = None
    for l in range(L):
        has_vres = l > 0
        w_l = w_in[l]
        if has_vres:
            pad = jnp.zeros((D, LANES - RW_VRES_RANK), F32)
            w_l = jnp.concatenate([w_l, w_vres[l - 1], pad], axis=1)
        proj = _inproj(x2, mod5, l, w_l.astype(BF16), S, has_vres)
        prw, qkv, ppool = proj[0], proj[1], proj[2]
        pvres = proj[3] if has_vres else None

        o1 = 0
        o2 = RW_DECAY_RANK
        o3 = RW_DECAY_RANK + RW_ICLR_RANK
        rp = {
            "mu": mu_rw[l].reshape(1, RW_COLS),
            "w0": rw_w0[l].reshape(1, W),
            "w2": _pad_rows(rw_w2[l], o1, RW_LORA).astype(BF16),
            "a0": rw_a0[l].reshape(1, W),
            "a2": _pad_rows(rw_a2[l], o2, RW_LORA).astype(BF16),
            "g2": _pad_rows(rw_g2[l], o3, RW_LORA).astype(BF16),
            "k_k": rw_k_k[l].reshape(1, W),
            "k_a": rw_k_a[l].reshape(1, W),
            "r_k": rw_r_k[l].reshape(1, W),
            "ln_g": rw_ln_g[l].reshape(1, W),
            "ln_b": rw_ln_b[l].reshape(1, W),
        }
        if has_vres:
            rp["mu_v"] = jnp.concatenate(
                [mu_vres[l - 1], jnp.zeros((LANES - RW_VRES_RANK,), F32)]).reshape(1, LANES)
            rp["v0"] = rw_v0[l - 1].reshape(1, W)
            rp["v2"] = _pad_rows(rw_v2[l - 1], 0, LANES).astype(BF16)
        rw_out, v_first = _rwkv(prw, pvres, v_first, rp, B, S)

        sb_out = _sb_attention(qkv, sb_norm_g[l].reshape(1, SB_WIDTH), B, S)

        n_pool = len(POOL_WINDOWS)
        w_blk = (jnp.eye(n_pool, dtype=F32)[:, None, :, None] * pool_w[l][:, :, None, :]
                 ).reshape(POOL_WIDTH, POOL_WIDTH).astype(BF16)
        pool_out = _pool(ppool, w_blk, pool_scale[l].reshape(1, POOL_WIDTH), B, S)

        wr_pad = jnp.concatenate([w_router[l], jnp.zeros((D, LANES - n_exp), F32)], axis=1)
        x2, h2, logits = _outproj(rw_out, sb_out, pool_out, x2, mod5, l, w_out[l].astype(BF16),
                                  wr_pad, S)
        gate, gate_t, rank_t = _router(logits, b_router[l].reshape(n_exp, 1), n_exp)
        x2 = _experts(h2, gate, gate_t, rank_t, x2, mod5, l, w_exp_gate, w_exp_up, w_exp_down,
                      w_sh_gate, w_sh_up, w_sh_down, g_final.reshape(1, D), S,
                      final_norm=(l == L - 1))
    return x2.reshape(B, S, D)
```

```python
import functools

import jax
import jax.numpy as jnp
from jax import lax
from jax.experimental import pallas as pl
from jax.experimental.pallas import tpu as pltpu

F32 = jnp.float32
BF16 = jnp.bfloat16

HEAD_DIM = 64
RW_HEADS = 6
RW_WIDTH = RW_HEADS * HEAD_DIM
SB_HEADS = 6
SB_WIDTH = SB_HEADS * HEAD_DIM
POOL_WINDOWS = (2, 4, 8, 16)
POOL_WIDTH = len(POOL_WINDOWS) * HEAD_DIM
RW_DECAY_RANK = 32
RW_ICLR_RANK = 32
RW_VRES_RANK = 32
RW_GATE_RANK = 64
RW_LORA = RW_DECAY_RANK + RW_ICLR_RANK + RW_GATE_RANK
RW_COLS = 3 * RW_WIDTH + RW_LORA
SB_COLS = 3 * SB_WIDTH
IN_COLS = RW_COLS + SB_COLS + POOL_WIDTH
RW_GN_EPS = 64e-5
RMS_EPS = 1e-6
N_GROUPS = 8
TOPK_GROUPS = 4
TOP_K = 6
ROUTED_SCALE = 2.5

LANES = 128
RW_CHUNK = 64
RW_SUB = 16
VMEM_LIMIT = 48 * 1024 * 1024
MOE_GROUP = 8
MOE_SUB = 256
MOE_CAP = 48
SB_DEAD_LOG = -104.0

HIGHEST = lax.Precision.HIGHEST


def _dot(a, b):
    return jnp.dot(a, b, preferred_element_type=F32)


def _bdot(a, b):
    return jnp.dot(a.astype(BF16), b.astype(BF16), preferred_element_type=F32)


def _bdot_nt(a, b):
    return lax.dot_general(a.astype(BF16), b.astype(BF16), (((1,), (1,)), ((), ())),
                           preferred_element_type=F32)


def _bdot_tn(a, b):
    return lax.dot_general(a.astype(BF16), b.astype(BF16), (((0,), (0,)), ((), ())),
                           preferred_element_type=F32)


def _split_dot(x, w, terms, left=False):
    acc = None
    rem = x
    for t in range(terms):
        part = rem.astype(BF16)
        d = _dot(w, part) if left else _dot(part, w)
        acc = d if acc is None else acc + d
        if t + 1 < terms:
            rem = rem - part.astype(F32)
    return acc


def _sigmoid(x):
    return 1.0 / (1.0 + jnp.exp(-x))


def _softplus(x):
    return jnp.maximum(x, 0.0) + jnp.log(1.0 + jnp.exp(-jnp.abs(x)))


def _params(*sem):
    return pltpu.CompilerParams(dimension_semantics=sem, vmem_limit_bytes=VMEM_LIMIT)


def _ada_kernel(c_ref, w_ref, b_ref, o_ref):
    c = c_ref[...]
    cond = c * _sigmoid(c)
    o_ref[...] = jnp.dot(cond, w_ref[...], precision=HIGHEST,
                         preferred_element_type=F32) + b_ref[...]


def _ada(c, w_ada, b_ada):
    L, D, N = w_ada.shape
    B = c.shape[0]
    tn = 1536 if N % 1536 == 0 else N
    return pl.pallas_call(
        _ada_kernel,
        grid=(L, N // tn),
        in_specs=[pl.BlockSpec((B, D), lambda l, n: (0, 0)),
                  pl.BlockSpec((None, D, tn), lambda l, n: (l, 0, n)),
                  pl.BlockSpec((None, 1, tn), lambda l, n: (l, 0, n))],
        out_specs=pl.BlockSpec((None, B, tn), lambda l, n: (l, 0, n)),
        out_shape=jax.ShapeDtypeStruct((L, B, N), F32),
        compiler_params=_params("arbitrary", "arbitrary"),
        name="ada_mod",
    )(c, w_ada, b_ada.reshape(L, 1, N))


def _mod_spec(l, j, tiles_per_batch, D):
    return pl.BlockSpec((None, None, None, 1, D),
                        lambda i, *_: (l, i // tiles_per_batch, j, 0, 0))


def _inproj_kernel(x_ref, sh_ref, sc_ref, w_ref, prw_ref, qkv_ref, pool_ref, *vres_ref):
    x = x_ref[...]
    ms = jnp.mean(x * x, axis=-1, keepdims=True)
    h = x * lax.rsqrt(ms + RMS_EPS) * (1.0 + sc_ref[...]) + sh_ref[...]
    hb = h.astype(BF16)
    prw_ref[...] = _dot(hb, w_ref[:, 0:RW_COLS])
    qkv_ref[...] = _dot(hb, w_ref[:, RW_COLS:RW_COLS + SB_COLS]).astype(BF16)
    pool_ref[...] = _dot(hb, w_ref[:, RW_COLS + SB_COLS:IN_COLS])
    if vres_ref:
        vres_ref[0][...] = _dot(hb, w_ref[:, IN_COLS:IN_COLS + LANES])


def _inproj(x2, mod5, l, w_bf, S, has_vres):
    T, D = x2.shape
    tm = min(512, S)
    tpb = S // tm
    n_w = w_bf.shape[1]
    out_shape = [jax.ShapeDtypeStruct((T, RW_COLS), F32),
                 jax.ShapeDtypeStruct((T, SB_COLS), BF16),
                 jax.ShapeDtypeStruct((T, POOL_WIDTH), F32)]
    out_specs = [pl.BlockSpec((tm, RW_COLS), lambda i: (i, 0)),
                 pl.BlockSpec((tm, SB_COLS), lambda i: (i, 0)),
                 pl.BlockSpec((tm, POOL_WIDTH), lambda i: (i, 0))]
    if has_vres:
        out_shape.append(jax.ShapeDtypeStruct((T, LANES), F32))
        out_specs.append(pl.BlockSpec((tm, LANES), lambda i: (i, 0)))
    return pl.pallas_call(
        _inproj_kernel,
        grid=(T // tm,),
        in_specs=[pl.BlockSpec((tm, D), lambda i: (i, 0)),
                  _mod_spec(l, 0, tpb, D), _mod_spec(l, 1, tpb, D),
                  pl.BlockSpec((D, n_w), lambda i: (0, 0))],
        out_specs=out_specs,
        out_shape=out_shape,
        compiler_params=_params("arbitrary"),
        name="in_proj",
    )(x2, mod5, mod5, w_bf)


def _unit_lower_solve(a_strict, rhs, eye, sub_mask):
    n_p = range(len(a_strict))
    ad = [jnp.where(sub_mask, a, 0.0) for a in a_strict]
    ao = [a_strict[i] - ad[i] for i in n_p]
    inv_d = [eye + ad[i] for i in n_p]
    pw = ad
    n = 2
    while n < RW_SUB:
        pw = [_bdot(pw[i], pw[i]) for i in n_p]
        inv_d = [inv_d[i] + _bdot(inv_d[i], pw[i]) for i in n_p]
        n *= 2
    m = [_bdot(inv_d[i], ao[i]) for i in n_p]
    t = [_bdot(inv_d[i], rhs[i]) for i in n_p]
    terms = []
    n = 1
    pw = m
    while n < RW_CHUNK // RW_SUB:
        terms.append(pw)
        n *= 2
        if n < RW_CHUNK // RW_SUB:
            pw = [_bdot(pw[i], pw[i]) for i in n_p]
    for pw in reversed(terms):
        t = [t[i] + _bdot(pw[i], t[i]) for i in n_p]
    return t


def _rwkv_kernel(*refs, has_vres, tm):
    if has_vres:
        (prw_ref, pvr_ref, vfirst_ref, mu_ref, w0_ref, w2_ref, a0_ref, a2_ref, g2_ref,
         kk_ref, ka_ref, rk_ref, lng_ref, lnb_ref, hblk_ref, ltri_ref,
         muv_ref, v0_ref, v2_ref,
         out_ref,
         carry_ref, carryv_ref, state_ref, at_ref, rt_ref, bt_ref, kt_ref, v_ref, cum_ref,
         y_ref, g_ref, bonus_ref) = refs
        vout_ref = None
    else:
        (prw_ref, mu_ref, w0_ref, w2_ref, a0_ref, a2_ref, g2_ref,
         kk_ref, ka_ref, rk_ref, lng_ref, lnb_ref, hblk_ref, ltri_ref,
         out_ref, vout_ref,
         carry_ref, state_ref, at_ref, rt_ref, bt_ref, kt_ref, v_ref, cum_ref,
         y_ref, g_ref, bonus_ref) = refs
        pvr_ref = vfirst_ref = muv_ref = v0_ref = v2_ref = carryv_ref = None

    s = pl.program_id(1)

    @pl.when(s == 0)
    def _():
        carry_ref[...] = jnp.zeros_like(carry_ref)
        state_ref[...] = jnp.zeros_like(state_ref)
        if has_vres:
            carryv_ref[...] = jnp.zeros_like(carryv_ref)

    row0 = lax.broadcasted_iota(jnp.int32, (tm, 1), 0) == 0

    def shift(p, c_ref, mu):
        prev = jnp.where(row0, c_ref[0:1, :], pltpu.roll(p, 1, 0))
        c_ref[0:1, :] = p[tm - 1:tm, :]
        return p + (prev - p) * mu

    W = RW_WIDTH
    f = shift(prw_ref[...], carry_ref, mu_ref[...])
    r = f[:, 0:W]
    k = f[:, W:2 * W]
    v = f[:, 2 * W:3 * W]
    lora = f[:, 3 * W:3 * W + RW_LORA]

    wlin = w0_ref[...] + _bdot(jnp.tanh(lora), w2_ref[...])
    lw = -jnp.exp(-_softplus(-wlin) - 0.5)
    a = _sigmoid(a0_ref[...] + _bdot(lora, a2_ref[...]))
    g_ref[...] = _bdot(_sigmoid(lora), g2_ref[...])
    if has_vres:
        fv = shift(pvr_ref[...], carryv_ref, muv_ref[...])
        v_gate = _sigmoid(v0_ref[...] + _bdot(fv, v2_ref[...]))
        v = v + (vfirst_ref[...] - v) * v_gate
    else:
        vout_ref[...] = v

    hblk = hblk_ref[...]
    kk = k * kk_ref[...]
    kk = kk / jnp.maximum(jnp.sqrt(_split_dot(kk * kk, hblk, 2)), 1e-12)
    k = k * (1.0 + (a - 1.0) * ka_ref[...])
    bonus_ref[...] = _split_dot(r * k * rk_ref[...], hblk, 2) * v

    cum = _split_dot(lw, ltri_ref[...], 3, left=True)
    p_inc = jnp.exp(cum)
    inv = jnp.exp(-cum)
    at_ref[...] = -kk * jnp.exp(cum - lw)
    rt_ref[...] = r * p_inc
    bt_ref[...] = kk * a * inv
    kt_ref[...] = k * inv
    v_ref[...] = v
    cum_ref[...] = cum

    C = RW_CHUNK
    ri = lax.broadcasted_iota(jnp.int32, (C, C), 0)
    ci = lax.broadcasted_iota(jnp.int32, (C, C), 1)
    strict = ri > ci
    incl = ri >= ci
    eye = (ri == ci).astype(F32)
    sub_mask = (ri // RW_SUB) == (ci // RW_SUB)

    def chunk(c, carry):
        r0 = pl.multiple_of(c * C, C)
        rows = pl.ds(r0, C)
        tail = pl.ds(pl.multiple_of(r0 + C - 8, 8), 8)
        hs = range(RW_HEADS)
        ls = [slice(h * HEAD_DIM, (h + 1) * HEAD_DIM) for h in hs]
        at = [at_ref[rows, ls[h]] for h in hs]
        rt = [rt_ref[rows, ls[h]] for h in hs]
        bt = [bt_ref[rows, ls[h]] for h in hs]
        kt = [kt_ref[rows, ls[h]] for h in hs]
        vv = [v_ref[rows, ls[h]] for h in hs]
        pc = [jnp.exp(cum_ref[tail, ls[h]][7:8, :]) for h in hs]
        st = [state_ref[h] for h in hs]
        a_ab = [jnp.where(strict, _bdot_nt(at[h], bt[h]), 0.0) for h in hs]
        a_ak = [jnp.where(strict, _bdot_nt(at[h], kt[h]), 0.0) for h in hs]
        a_rb = [jnp.where(incl, _bdot_nt(rt[h], bt[h]), 0.0) for h in hs]
        a_rk = [jnp.where(incl, _bdot_nt(rt[h], kt[h]), 0.0) for h in hs]
        a_s = [_bdot_nt(at[h], st[h]) for h in hs]
        r_s = [_bdot_nt(rt[h], st[h]) for h in hs]
        rhs = [a_s[h] + _bdot(a_ak[h], vv[h]) for h in hs]
        y0 = [r_s[h] + _bdot(a_rk[h], vv[h]) for h in hs]
        s0 = [st[h] * pc[h] + _bdot_tn(vv[h], kt[h] * pc[h]) for h in hs]
        z = _unit_lower_solve(a_ab, rhs, eye, sub_mask)
        y = [y0[h] + _bdot(a_rb[h], z[h]) for h in hs]
        s1 = [s0[h] + _bdot_tn(z[h], bt[h] * pc[h]) for h in hs]
        for h in hs:
            y_ref[rows, ls[h]] = y[h]
            state_ref[h] = s1[h]
        return carry

    lax.fori_loop(0, tm // C, chunk, 0)

    y = y_ref[...]
    inv_n = 1.0 / HEAD_DIM
    mean = _split_dot(y, hblk, 2) * inv_n
    d = y - mean
    var = _split_dot(d * d, hblk, 2) * inv_n
    yn = d * lax.rsqrt(var + RW_GN_EPS) * lng_ref[...] + lnb_ref[...]
    out_ref[...] = ((yn + bonus_ref[...]) * g_ref[...]).astype(out_ref.dtype)


def _rwkv(prw, pvres, v_first, p, B, S):
    T = prw.shape[0]
    tm = min(512, S)
    ns = S // tm
    has_vres = pvres is not None
    W = RW_WIDTH
    row = lambda b, s: (b * ns + s, 0)
    const = lambda b, s: (0, 0)
    vec = lambda n: pl.BlockSpec((1, n), const)
    mat = lambda m, n: pl.BlockSpec((m, n), const)

    hblk = jnp.kron(jnp.eye(RW_HEADS, dtype=F32), jnp.ones((HEAD_DIM, HEAD_DIM), F32)).astype(BF16)
    ltri = jnp.kron(jnp.eye(tm // RW_CHUNK, dtype=F32),
                    jnp.tril(jnp.ones((RW_CHUNK, RW_CHUNK), F32))).astype(BF16)

    inputs = [prw]
    in_specs = [pl.BlockSpec((tm, RW_COLS), row)]
    if has_vres:
        inputs += [pvres, v_first]
        in_specs += [pl.BlockSpec((tm, LANES), row), pl.BlockSpec((tm, W), row)]
    inputs += [p["mu"], p["w0"], p["w2"], p["a0"], p["a2"], p["g2"], p["k_k"], p["k_a"],
               p["r_k"], p["ln_g"], p["ln_b"], hblk, ltri]
    in_specs += [vec(RW_COLS), vec(W), mat(RW_LORA, W), vec(W), mat(RW_LORA, W), mat(RW_LORA, W),
                 vec(W), vec(W), vec(W), vec(W), vec(W), mat(W, W), mat(tm, tm)]
    if has_vres:
        inputs += [p["mu_v"], p["v0"], p["v2"]]
        in_specs += [vec(LANES), vec(W), mat(LANES, W)]

    out_shape = [jax.ShapeDtypeStruct((T, W), BF16)]
    out_specs = [pl.BlockSpec((tm, W), row)]
    if not has_vres:
        out_shape.append(jax.ShapeDtypeStruct((T, W), F32))
        out_specs.append(pl.BlockSpec((tm, W), row))

    big = lambda: pltpu.VMEM((tm, W), F32)
    scratch = [pltpu.VMEM((8, RW_COLS), F32)]
    if has_vres:
        scratch.append(pltpu.VMEM((8, LANES), F32))
    scratch += [pltpu.VMEM((RW_HEADS, HEAD_DIM, HEAD_DIM), F32)] + [big() for _ in range(9)]

    res = pl.pallas_call(
        functools.partial(_rwkv_kernel, has_vres=has_vres, tm=tm),
        grid=(B, ns),
        in_specs=in_specs,
        out_specs=out_specs,
        out_shape=out_shape,
        scratch_shapes=scratch,
        compiler_params=_params("arbitrary", "arbitrary"),
        name="rwkv7_mix",
    )(*inputs)
    return (res[0], v_first) if has_vres else (res[0], res[1])


def _sb_kernel(q_ref, k_ref, v_ref, g_ref, o_ref, *, tq):
    i = pl.program_id(2)
    lane = lax.broadcasted_iota(jnp.int32, (1, LANES), 1)
    head_of_lane = lane // HEAD_DIM
    ri = lax.broadcasted_iota(jnp.int32, (tq, tq), 0)
    ci = lax.broadcasted_iota(jnp.int32, (tq, tq), 1)
    causal = ri > ci
    u_incl = (ri >= ci).astype(BF16)
    scale = HEAD_DIM ** -0.5
    q_all = q_ref[...]
    heads = range(2)
    qh = [jnp.where(head_of_lane == h, q_all, jnp.zeros_like(q_all)) * jnp.asarray(scale, BF16)
          for h in heads]

    def tile(j, acc, run, diagonal):
        k0 = pl.multiple_of(j * tq, tq)
        kt = k_ref[pl.ds(k0, tq), :]
        vt = v_ref[pl.ds(k0, tq), :]
        z = [lax.dot_general(qh[h], kt, (((1,), (1,)), ((), ())), preferred_element_type=F32)
             for h in heads]
        ln = [-_softplus(z[h]) for h in heads]
        lnm = [jnp.where(causal, ln[h], 0.0) for h in heads] if diagonal else ln
        inc = [_split_dot(lnm[h], u_incl, 2) for h in heads]
        att = [jnp.exp(z[h] + ln[h] + (inc[h] - lnm[h]) + run[h]) for h in heads]
        if diagonal:
            att = [jnp.where(causal, att[h], 0.0) for h in heads]
        acc = [acc[h] + _dot(att[h].astype(BF16), vt) for h in heads]
        run = [run[h] + inc[h][:, 0:1] for h in heads]
        return acc, run

    acc, run = tile(i, [jnp.zeros((tq, LANES), F32)] * 2, [jnp.zeros((tq, 1), F32)] * 2, True)

    def alive(run):
        return jnp.max(jnp.maximum(run[0], run[1]))

    def cond(carry):
        j, _, _, top = carry
        return jnp.logical_and(j >= 0, top > SB_DEAD_LOG)

    def body(carry):
        j, acc, run, _ = carry
        acc, run = tile(j, acc, run, False)
        return j - 1, acc, run, alive(run)

    _, accs, _, _ = lax.while_loop(cond, body, (i - 1, acc, run, alive(run)))

    o = jnp.where(head_of_lane == 0, accs[0], accs[1])
    o2 = o * o
    s0 = jnp.sum(jnp.where(head_of_lane == 0, o2, 0.0), axis=-1, keepdims=True)
    s1 = jnp.sum(o2, axis=-1, keepdims=True) - s0
    ms = jnp.where(head_of_lane == 0, s0, s1) * (1.0 / HEAD_DIM)
    o_ref[...] = (o * lax.rsqrt(ms + RMS_EPS) * g_ref[...]).astype(o_ref.dtype)


def _sb_attention(qkv, g, B, S):
    T = qkv.shape[0]
    tq = min(256, S)
    nq = S // tq
    npair = SB_WIDTH // LANES
    return pl.pallas_call(
        functools.partial(_sb_kernel, tq=tq),
        grid=(B, npair, nq),
        in_specs=[pl.BlockSpec((tq, LANES), lambda b, p, i: (b * nq + i, p)),
                  pl.BlockSpec((S, LANES), lambda b, p, i: (b, npair + p)),
                  pl.BlockSpec((S, LANES), lambda b, p, i: (b, 2 * npair + p)),
                  pl.BlockSpec((1, LANES), lambda b, p, i: (0, p))],
        out_specs=pl.BlockSpec((tq, LANES), lambda b, p, i: (b * nq + i, p)),
        out_shape=jax.ShapeDtypeStruct((T, SB_WIDTH), BF16),
        compiler_params=_params("arbitrary", "arbitrary", "arbitrary"),
        name="stickbreak_attn",
    )(qkv, qkv, qkv, g)


def _pool_kernel(x_ref, w_ref, sc_ref, o_ref, *, S):
    x = x_ref[...]
    pos = lax.broadcasted_iota(jnp.int32, (S, 1), 0)
    group = lax.broadcasted_iota(jnp.int32, (1, POOL_WIDTH), 1) // HEAD_DIM

    sums = []
    acc = x
    step = 1
    for _ in POOL_WINDOWS:
        acc = acc + jnp.where(pos >= step, pltpu.roll(acc, step, 0), 0.0)
        step *= 2
        sums.append(acc)
    sel = sums[-1]
    win = jnp.full((1, POOL_WIDTH), float(POOL_WINDOWS[-1]), F32)
    for gi in range(len(POOL_WINDOWS) - 2, -1, -1):
        sel = jnp.where(group == gi, sums[gi], sel)
        win = jnp.where(group == gi, float(POOL_WINDOWS[gi]), win)
    count = jnp.minimum((pos + 1).astype(F32), win)
    pooled = sel / count - x
    o_ref[...] = (_bdot(pooled, w_ref[...]) * sc_ref[...]).astype(o_ref.dtype)


def _pool(ppool, w_blk, scale, B, S):
    T = ppool.shape[0]
    return pl.pallas_call(
        functools.partial(_pool_kernel, S=S),
        grid=(B,),
        in_specs=[pl.BlockSpec((S, POOL_WIDTH), lambda b: (b, 0)),
                  pl.BlockSpec((POOL_WIDTH, POOL_WIDTH), lambda b: (0, 0)),
                  pl.BlockSpec((1, POOL_WIDTH), lambda b: (0, 0))],
        out_specs=pl.BlockSpec((S, POOL_WIDTH), lambda b: (b, 0)),
        out_shape=jax.ShapeDtypeStruct((T, POOL_WIDTH), BF16),
        compiler_params=_params("arbitrary"),
        name="multiscale_pool",
    )(ppool, w_blk, scale)


def _outproj_kernel(rw_ref, sb_ref, pool_ref, x_ref, gt_ref, sh_ref, sc_ref, w_ref, wr_ref,
                    xo_ref, h_ref, lg_ref):
    a = RW_WIDTH
    b = RW_WIDTH + SB_WIDTH
    mixed = (_dot(rw_ref[...], w_ref[0:a, :]) + _dot(sb_ref[...], w_ref[a:b, :])
             + _dot(pool_ref[...], w_ref[b:, :]))
    x = x_ref[...] + gt_ref[...] * mixed
    xo_ref[...] = x
    ms = jnp.mean(x * x, axis=-1, keepdims=True)
    h = x * lax.rsqrt(ms + RMS_EPS) * (1.0 + sc_ref[...]) + sh_ref[...]
    h_ref[...] = h.astype(BF16)
    lg_ref[...] = jnp.dot(h, wr_ref[...], precision=HIGHEST, preferred_element_type=F32)


def _outproj(rw, sb, pool, x2, mod5, l, w_bf, wr_pad, S):
    T, D = x2.shape
    tm = min(512, S)
    tpb = S // tm
    row = lambda n: pl.BlockSpec((tm, n), lambda i: (i, 0))
    return pl.pallas_call(
        _outproj_kernel,
        grid=(T // tm,),
        in_specs=[row(RW_WIDTH), row(SB_WIDTH), row(POOL_WIDTH), row(D),
                  _mod_spec(l, 2, tpb, D), _mod_spec(l, 3, tpb, D), _mod_spec(l, 4, tpb, D),
                  pl.BlockSpec((D, D), lambda i: (0, 0)),
                  pl.BlockSpec((D, LANES), lambda i: (0, 0))],
        out_specs=[row(D), row(D), row(LANES)],
        out_shape=[jax.ShapeDtypeStruct((T, D), F32), jax.ShapeDtypeStruct((T, D), BF16),
                   jax.ShapeDtypeStruct((T, LANES), F32)],
        compiler_params=_params("arbitrary"),
        name="out_proj",
    )(rw, sb, pool, x2, mod5, mod5, mod5, w_bf, wr_pad)


def _router_kernel(lg_ref, b_ref, u_ref, gate_ref, gatet_ref, rank_ref, *, n_exp, tm):
    per_group = n_exp // N_GROUPS
    neg = -jnp.inf
    lt = lg_ref[...].T[0:n_exp, :]
    scores = _sigmoid(lt)
    sel = scores + b_ref[...]
    sel3 = sel.reshape(N_GROUPS, per_group, tm)
    i_in = lax.broadcasted_iota(jnp.int32, sel3.shape, 1)
    m1 = jnp.max(sel3, axis=1, keepdims=True)
    first = jnp.min(jnp.where(sel3 == m1, i_in, per_group), axis=1, keepdims=True)
    m2 = jnp.max(jnp.where(i_in == first, neg, sel3), axis=1, keepdims=True)
    gs = m1 + m2
    gi = lax.broadcasted_iota(jnp.int32, gs.shape, 0)
    grank = jnp.zeros(gs.shape, jnp.int32)
    for g2 in range(N_GROUPS):
        o = gs[g2:g2 + 1]
        beats = jnp.where(o > gs, 1, jnp.where(o == gs, jnp.where(gi > g2, 1, 0), 0))
        grank = grank + beats
    masked = jnp.where(grank < TOPK_GROUPS, sel3, neg).reshape(n_exp, tm)
    ei = lax.broadcasted_iota(jnp.int32, masked.shape, 0)
    rank = jnp.zeros(masked.shape, jnp.int32)
    for e2 in range(n_exp):
        o = masked[e2:e2 + 1, :]
        beats = jnp.where(o > masked, 1, jnp.where(o == masked, jnp.where(ei > e2, 1, 0), 0))
        rank = rank + beats
    w = jnp.where(rank < TOP_K, scores, 0.0)
    denom = jnp.sum(w, axis=0, keepdims=True)
    gate = w / (denom + 1e-20) * ROUTED_SCALE
    gatet_ref[...] = gate
    routed = jnp.where(gate != 0.0, 1.0, 0.0).astype(BF16)
    rank_ref[...] = _dot(routed, u_ref[...]).astype(jnp.int32)
    gate = jnp.concatenate([gate, jnp.zeros((LANES - n_exp, tm), F32)], axis=0)
    gate_ref[...] = gate.T


def _router(logits, b_col, n_exp):
    T = logits.shape[0]
    tm = min(512, T)
    t_i = jnp.arange(tm)
    before = (t_i[:, None] < t_i[None, :]) & (t_i[:, None] // MOE_SUB == t_i[None, :] // MOE_SUB)
    return pl.pallas_call(
        functools.partial(_router_kernel, n_exp=n_exp, tm=tm),
        grid=(T // tm,),
        in_specs=[pl.BlockSpec((tm, LANES), lambda i: (i, 0)),
                  pl.BlockSpec((n_exp, 1), lambda i: (0, 0)),
                  pl.BlockSpec((tm, tm), lambda i: (0, 0))],
        out_specs=[pl.BlockSpec((tm, LANES), lambda i: (i, 0)),
                   pl.BlockSpec((n_exp, tm), lambda i: (0, i)),
                   pl.BlockSpec((n_exp, tm), lambda i: (0, i))],
        out_shape=[jax.ShapeDtypeStruct((T, LANES), F32),
                   jax.ShapeDtypeStruct((n_exp, T), F32),
                   jax.ShapeDtypeStruct((n_exp, T), jnp.int32)],
        compiler_params=_params("arbitrary"),
        name="router_topk",
    )(logits, b_col, before.astype(BF16))


def _experts_kernel(h_ref, gate_ref, gatet_ref, rank_ref, x_ref, gt_ref, wg_ref, wu_ref, wd_ref,
                    sg_ref, su_ref, sd_ref, gfin_ref, o_ref, xs_ref, ys_ref, pw_ref, flag_ref,
                    *, n_exp, final_norm, tm):
    e = pl.program_id(1)
    el = lax.rem(e, MOE_GROUP)
    n_sub = tm // MOE_SUB
    routed = e < n_exp

    @pl.when(e == 0)
    def _():
        o_ref[...] = jnp.zeros_like(o_ref)

    def swiglu(x, wg, wu):
        a = _dot(x, wg[...].astype(BF16))
        u = _dot(x, wu[...].astype(BF16))
        return a * _sigmoid(a) * u

    @pl.when(jnp.logical_and(routed, el == 0))
    def _group_start():
        g0 = pl.multiple_of(e, MOE_GROUP)
        gate_g = gatet_ref[pl.ds(g0, MOE_GROUP), :]
        rank_g = rank_ref[pl.ds(g0, MOE_GROUP), :]
        sel = gate_g != 0.0
        worst = jnp.zeros((MOE_GROUP, 1), F32)
        for s in range(n_sub):
            cols = slice(s * MOE_SUB, (s + 1) * MOE_SUB)
            worst = jnp.maximum(
                worst, jnp.sum(jnp.where(sel[:, cols], 1.0, 0.0), axis=1, keepdims=True))
        fits = worst <= float(MOE_CAP)
        for j in range(MOE_GROUP):
            flag_ref[j] = jnp.where(worst[j, 0] <= float(MOE_CAP), 1, 0).astype(jnp.int32)
        sel_fit = jnp.logical_and(sel, fits)
        slot = lax.broadcasted_iota(jnp.int32, (MOE_CAP, 1), 0)
        for s in range(n_sub):
            cols = slice(s * MOE_SUB, (s + 1) * MOE_SUB)
            ps, pws = [], []
            for j in range(MOE_GROUP):
                hit = jnp.logical_and(sel_fit[j:j + 1, cols], rank_g[j:j + 1, cols] == slot)
                ps.append(jnp.where(hit, 1.0, 0.0))
                pws.append(jnp.where(hit, gate_g[j:j + 1, cols], 0.0))
            p = jnp.concatenate(ps, axis=0).astype(BF16)
            pw_ref[s] = jnp.concatenate(pws, axis=0).astype(BF16)
            rows = _dot(p, h_ref[s * MOE_SUB:(s + 1) * MOE_SUB, :]).astype(BF16)
            for j in range(MOE_GROUP):
                xs_ref[j, s * MOE_CAP:(s + 1) * MOE_CAP, :] = rows[j * MOE_CAP:(j + 1) * MOE_CAP]

    compact = flag_ref[el] == 1

    @pl.when(jnp.logical_and(routed, compact))
    def _():
        act = swiglu(xs_ref[el], wg_ref, wu_ref)
        y = _dot(act.astype(BF16), wd_ref[...].astype(BF16)).astype(BF16)
        for s in range(n_sub):
            ys_ref[s, el] = y[s * MOE_CAP:(s + 1) * MOE_CAP]

    @pl.when(jnp.logical_and(routed, jnp.logical_not(compact)))
    def _():
        lane = lax.broadcasted_iota(jnp.int32, (1, LANES), 1)
        gcol = jnp.sum(jnp.where(lane == e, gate_ref[...], 0.0), axis=-1, keepdims=True)
        act = swiglu(h_ref[...], wg_ref, wu_ref) * gcol
        o_ref[...] += _dot(act.astype(BF16), wd_ref[...].astype(BF16))
        for s in range(n_sub):
            ys_ref[s, el] = jnp.zeros((MOE_CAP, o_ref.shape[1]), BF16)

    @pl.when(jnp.logical_and(routed, el == MOE_GROUP - 1))
    def _scatter():
        for s in range(n_sub):
            y_s = jnp.concatenate([ys_ref[s, j] for j in range(MOE_GROUP)], axis=0)
            o_ref[s * MOE_SUB:(s + 1) * MOE_SUB, :] += lax.dot_general(
                pw_ref[s], y_s, (((0,), (0,)), ((), ())), preferred_element_type=F32)

    @pl.when(e == n_exp)
    def _():
        act = swiglu(h_ref[...], sg_ref, su_ref)
        x = x_ref[...] + gt_ref[...] * (o_ref[...] + _dot(act.astype(BF16), sd_ref[...].astype(BF16)))
        if final_norm:
            ms = jnp.mean(x * x, axis=-1, keepdims=True)
            x = x * lax.rsqrt(ms + RMS_EPS) * gfin_ref[...]
        o_ref[...] = x


def _experts(h2, gate, gate_t, rank_t, x2, mod5, l, w_gate, w_up, w_down, s_gate, s_up, s_down,
             g_final, S, final_norm):
    T, D = x2.shape
    n_exp, _, F = w_gate.shape[1:]
    assert n_exp % MOE_GROUP == 0
    tm = min(1024, S)
    tpb = S // tm
    n_sub = tm // MOE_SUB
    ex = lambda e: jnp.minimum(e, n_exp - 1)
    return pl.pallas_call(
        functools.partial(_experts_kernel, n_exp=n_exp, final_norm=final_norm, tm=tm),
        grid=(T // tm, n_exp + 1),
        in_specs=[pl.BlockSpec((tm, D), lambda i, e: (i, 0)),
                  pl.BlockSpec((tm, LANES), lambda i, e: (i, 0)),
                  pl.BlockSpec((n_exp, tm), lambda i, e: (0, i)),
                  pl.BlockSpec((n_exp, tm), lambda i, e: (0, i)),
                  pl.BlockSpec((tm, D), lambda i, e: (i, 0)),
                  _mod_spec(l, 5, tpb, D),
                  pl.BlockSpec((None, None, D, F), lambda i, e: (l, ex(e), 0, 0)),
                  pl.BlockSpec((None, None, D, F), lambda i, e: (l, ex(e), 0, 0)),
                  pl.BlockSpec((None, None, F, D), lambda i, e: (l, ex(e), 0, 0)),
                  pl.BlockSpec((None, D, F), lambda i, e: (l, 0, 0)),
                  pl.BlockSpec((None, D, F), lambda i, e: (l, 0, 0)),
                  pl.BlockSpec((None, F, D), lambda i, e: (l, 0, 0)),
                  pl.BlockSpec((1, D), lambda i, e: (0, 0))],
        out_specs=pl.BlockSpec((tm, D), lambda i, e: (i, 0)),
        out_shape=jax.ShapeDtypeStruct((T, D), F32),
        scratch_shapes=[pltpu.VMEM((MOE_GROUP, n_sub * MOE_CAP, D), BF16),
                        pltpu.VMEM((n_sub, MOE_GROUP, MOE_CAP, D), BF16),
                        pltpu.VMEM((n_sub, MOE_GROUP * MOE_CAP, MOE_SUB), BF16),
                        pltpu.SMEM((MOE_GROUP,), jnp.int32)],
        compiler_params=_params("arbitrary", "arbitrary"),
        name="experts_ffn",
    )(h2, gate, gate_t, rank_t, x2, mod5, w_gate, w_up, w_down, s_gate, s_up, s_down, g_final)


def _pad_rows(w, lo, n_rows):
    return jnp.zeros((n_rows, w.shape[1]), w.dtype).at[lo:lo + w.shape[0]].set(w)


def kernel(x, c, w_ada, b_ada, w_in, w_vres, mu_rw, mu_vres, rw_w0, rw_w2, rw_a0, rw_a2, rw_v0, rw_v2, rw_g2, rw_k_k, rw_k_a, rw_r_k, rw_ln_g, rw_ln_b, sb_norm_g, pool_w, pool_scale, w_out, w_router, b_router, w_exp_gate, w_exp_up, w_exp_down, w_sh_gate, w_sh_up, w_sh_down, g_final):
    B, S, D = x.shape
    L = w_in.shape[0]
    T = B * S
    n_exp = w_router.shape[2]
    W = RW_WIDTH

    mod = _ada(c, w_ada, b_ada)
    mod5 = mod.reshape(L, B, 6, 1, D)

    x2 = x.reshape(T, D)
    v_first = None
    for l in range(L):
        has_vres = l > 0
        w_l = w_in[l]
        if has_vres:
            pad = jnp.zeros((D, LANES - RW_VRES_RANK), F32)
            w_l = jnp.concatenate([w_l, w_vres[l - 1], pad], axis=1)
        proj = _inproj(x2, mod5, l, w_l.astype(BF16), S, has_vres)
        prw, qkv, ppool = proj[0], proj[1], proj[2]
        pvres = proj[3] if has_vres else None

        o1 = 0
        o2 = RW_DECAY_RANK
        o3 = RW_DECAY_RANK + RW_ICLR_RANK
        rp = {
            "mu": mu_rw[l].reshape(1, RW_COLS),
            "w0": rw_w0[l].reshape(1, W),
            "w2": _pad_rows(rw_w2[l], o1, RW_LORA).astype(BF16),
            "a0": rw_a0[l].reshape(1, W),
            "a2": _pad_rows(rw_a2[l], o2, RW_LORA).astype(BF16),
            "g2": _pad_rows(rw_g2[l], o3, RW_LORA).astype(BF16),
            "k_k": rw_k_k[l].reshape(1, W),
            "k_a": rw_k_a[l].reshape(1, W),
            "r_k": rw_r_k[l].reshape(1, W),
            "ln_g": rw_ln_g[l].reshape(1, W),
            "ln_b": rw_ln_b[l].reshape(1, W),
        }
        if has_vres:
            rp["mu_v"] = jnp.concatenate(
                [mu_vres[l - 1], jnp.zeros((LANES - RW_VRES_RANK,), F32)]).reshape(1, LANES)
            rp["v0"] = rw_v0[l - 1].reshape(1, W)
            rp["v2"] = _pad_rows(rw_v2[l - 1], 0, LANES).astype(BF16)
        rw_out, v_first = _rwkv(prw, pvres, v_first, rp, B, S)

        sb_out = _sb_attention(qkv, sb_norm_g[l].reshape(1, SB_WIDTH), B, S)

        n_pool = len(POOL_WINDOWS)
        w_blk = (jnp.eye(n_pool, dtype=F32)[:, None, :, None] * pool_w[l][:, :, None, :]
                 ).reshape(POOL_WIDTH, POOL_WIDTH).astype(BF16)
        pool_out = _pool(ppool, w_blk, pool_scale[l].reshape(1, POOL_WIDTH), B, S)

        wr_pad = jnp.concatenate([w_router[l], jnp.zeros((D, LANES - n_exp), F32)], axis=1)
        x2, h2, logits = _outproj(rw_out, sb_out, pool_out, x2, mod5, l, w_out[l].astype(BF16),
                                  wr_pad, S)
        gate, gate_t, rank_t = _router(logits, b_router[l].reshape(n_exp, 1), n_exp)
        x2 = _experts(h2, gate, gate_t, rank_t, x2, mod5, l, w_exp_gate, w_exp_up, w_exp_down,
                      w_sh_gate, w_sh_up, w_sh_down, g_final.reshape(1, D), S,
                      final_norm=(l == L - 1))
    return x2.reshape(B, S, D)
```

```python
import functools

import jax
import jax.numpy as jnp
from jax import lax
from jax.experimental import pallas as pl
from jax.experimental.pallas import tpu as pltpu

F32 = jnp.float32
BF16 = jnp.bfloat16

HEAD_DIM = 64
RW_HEADS = 6
RW_WIDTH = RW_HEADS * HEAD_DIM
SB_HEADS = 6
SB_WIDTH = SB_HEADS * HEAD_DIM
POOL_WINDOWS = (2, 4, 8, 16)
POOL_WIDTH = len(POOL_WINDOWS) * HEAD_DIM
RW_DECAY_RANK = 32
RW_ICLR_RANK = 32
RW_VRES_RANK = 32
RW_GATE_RANK = 64
RW_LORA = RW_DECAY_RANK + RW_ICLR_RANK + RW_GATE_RANK
RW_COLS = 3 * RW_WIDTH + RW_LORA
SB_COLS = 3 * SB_WIDTH
IN_COLS = RW_COLS + SB_COLS + POOL_WIDTH
RW_GN_EPS = 64e-5
RMS_EPS = 1e-6
N_GROUPS = 8
TOPK_GROUPS = 4
TOP_K = 6
ROUTED_SCALE = 2.5

LANES = 128
RW_CHUNK = 64
RW_SUB = 16
VMEM_LIMIT = 48 * 1024 * 1024
MOE_GROUP = 8
MOE_SUB = 256
MOE_CAP = 64
SB_DEAD_LOG = -104.0

HIGHEST = lax.Precision.HIGHEST


def _dot(a, b):
    return jnp.dot(a, b, preferred_element_type=F32)


def _bdot(a, b):
    return jnp.dot(a.astype(BF16), b.astype(BF16), preferred_element_type=F32)


def _bdot_nt(a, b):
    return lax.dot_general(a.astype(BF16), b.astype(BF16), (((1,), (1,)), ((), ())),
                           preferred_element_type=F32)


def _bdot_tn(a, b):
    return lax.dot_general(a.astype(BF16), b.astype(BF16), (((0,), (0,)), ((), ())),
                           preferred_element_type=F32)


def _split_dot(x, w, terms, left=False):
    acc = None
    rem = x
    for t in range(terms):
        part = rem.astype(BF16)
        d = _dot(w, part) if left else _dot(part, w)
        acc = d if acc is None else acc + d
        if t + 1 < terms:
            rem = rem - part.astype(F32)
    return acc


def _sigmoid(x):
    return 1.0 / (1.0 + jnp.exp(-x))


def _softplus(x):
    return jnp.maximum(x, 0.0) + jnp.log(1.0 + jnp.exp(-jnp.abs(x)))


def _params(*sem):
    return pltpu.CompilerParams(dimension_semantics=sem, vmem_limit_bytes=VMEM_LIMIT)


def _ada_kernel(c_ref, w_ref, b_ref, o_ref):
    c = c_ref[...]
    cond = c * _sigmoid(c)
    o_ref[...] = jnp.dot(cond, w_ref[...], precision=HIGHEST,
                         preferred_element_type=F32) + b_ref[...]


def _ada(c, w_ada, b_ada):
    L, D, N = w_ada.shape
    B = c.shape[0]
    tn = 1536 if N % 1536 == 0 else N
    return pl.pallas_call(
        _ada_kernel,
        grid=(L, N // tn),
        in_specs=[pl.BlockSpec((B, D), lambda l, n: (0, 0)),
                  pl.BlockSpec((None, D, tn), lambda l, n: (l, 0, n)),
                  pl.BlockSpec((None, 1, tn), lambda l, n: (l, 0, n))],
        out_specs=pl.BlockSpec((None, B, tn), lambda l, n: (l, 0, n)),
        out_shape=jax.ShapeDtypeStruct((L, B, N), F32),
        compiler_params=_params("arbitrary", "arbitrary"),
        name="ada_mod",
    )(c, w_ada, b_ada.reshape(L, 1, N))


def _mod_spec(l, j, tiles_per_batch, D):
    return pl.BlockSpec((None, None, None, 1, D),
                        lambda i, *_: (l, i // tiles_per_batch, j, 0, 0))


def _inproj_kernel(x_ref, sh_ref, sc_ref, w_ref, prw_ref, qkv_ref, pool_ref, *vres_ref):
    x = x_ref[...]
    ms = jnp.mean(x * x, axis=-1, keepdims=True)
    h = x * lax.rsqrt(ms + RMS_EPS) * (1.0 + sc_ref[...]) + sh_ref[...]
    hb = h.astype(BF16)
    prw_ref[...] = _dot(hb, w_ref[:, 0:RW_COLS])
    qkv_ref[...] = _dot(hb, w_ref[:, RW_COLS:RW_COLS + SB_COLS]).astype(BF16)
    pool_ref[...] = _dot(hb, w_ref[:, RW_COLS + SB_COLS:IN_COLS])
    if vres_ref:
        vres_ref[0][...] = _dot(hb, w_ref[:, IN_COLS:IN_COLS + LANES])


def _inproj(x2, mod5, l, w_bf, S, has_vres):
    T, D = x2.shape
    tm = min(512, S)
    tpb = S // tm
    n_w = w_bf.shape[1]
    out_shape = [jax.ShapeDtypeStruct((T, RW_COLS), F32),
                 jax.ShapeDtypeStruct((T, SB_COLS), BF16),
                 jax.ShapeDtypeStruct((T, POOL_WIDTH), F32)]
    out_specs = [pl.BlockSpec((tm, RW_COLS), lambda i: (i, 0)),
                 pl.BlockSpec((tm, SB_COLS), lambda i: (i, 0)),
                 pl.BlockSpec((tm, POOL_WIDTH), lambda i: (i, 0))]
    if has_vres:
        out_shape.append(jax.ShapeDtypeStruct((T, LANES), F32))
        out_specs.append(pl.BlockSpec((tm, LANES), lambda i: (i, 0)))
    return pl.pallas_call(
        _inproj_kernel,
        grid=(T // tm,),
        in_specs=[pl.BlockSpec((tm, D), lambda i: (i, 0)),
                  _mod_spec(l, 0, tpb, D), _mod_spec(l, 1, tpb, D),
                  pl.BlockSpec((D, n_w), lambda i: (0, 0))],
        out_specs=out_specs,
        out_shape=out_shape,
        compiler_params=_params("arbitrary"),
        name="in_proj",
    )(x2, mod5, mod5, w_bf)


def _unit_lower_solve(a_strict, rhs, eye, sub_mask):
    n_p = range(len(a_strict))
    ad = [jnp.where(sub_mask, a, 0.0) for a in a_strict]
    ao = [a_strict[i] - ad[i] for i in n_p]
    inv_d = [eye + ad[i] for i in n_p]
    pw = ad
    n = 2
    while n < RW_SUB:
        pw = [_bdot(pw[i], pw[i]) for i in n_p]
        inv_d = [inv_d[i] + _bdot(inv_d[i], pw[i]) for i in n_p]
        n *= 2
    m = [_bdot(inv_d[i], ao[i]) for i in n_p]
    t = [_bdot(inv_d[i], rhs[i]) for i in n_p]
    terms = []
    n = 1
    pw = m
    while n < RW_CHUNK // RW_SUB:
        terms.append(pw)
        n *= 2
        if n < RW_CHUNK // RW_SUB:
            pw = [_bdot(pw[i], pw[i]) for i in n_p]
    for pw in reversed(terms):
        t = [t[i] + _bdot(pw[i], t[i]) for i in n_p]
    return t


def _rwkv_kernel(*refs, has_vres, tm):
    if has_vres:
        (prw_ref, pvr_ref, vfirst_ref, mu_ref, w0_ref, w2_ref, a0_ref, a2_ref, g2_ref,
         kk_ref, ka_ref, rk_ref, lng_ref, lnb_ref, hblk_ref, ltri_ref,
         muv_ref, v0_ref, v2_ref,
         out_ref,
         carry_ref, carryv_ref, state_ref, at_ref, rt_ref, bt_ref, kt_ref, v_ref, cum_ref,
         y_ref, g_ref, bonus_ref) = refs
        vout_ref = None
    else:
        (prw_ref, mu_ref, w0_ref, w2_ref, a0_ref, a2_ref, g2_ref,
         kk_ref, ka_ref, rk_ref, lng_ref, lnb_ref, hblk_ref, ltri_ref,
         out_ref, vout_ref,
         carry_ref, state_ref, at_ref, rt_ref, bt_ref, kt_ref, v_ref, cum_ref,
         y_ref, g_ref, bonus_ref) = refs
        pvr_ref = vfirst_ref = muv_ref = v0_ref = v2_ref = carryv_ref = None

    s = pl.program_id(1)

    @pl.when(s == 0)
    def _():
        carry_ref[...] = jnp.zeros_like(carry_ref)
        state_ref[...] = jnp.zeros_like(state_ref)
        if has_vres:
            carryv_ref[...] = jnp.zeros_like(carryv_ref)

    row0 = lax.broadcasted_iota(jnp.int32, (tm, 1), 0) == 0

    def shift(p, c_ref, mu):
        prev = jnp.where(row0, c_ref[0:1, :], pltpu.roll(p, 1, 0))
        c_ref[0:1, :] = p[tm - 1:tm, :]
        return p + (prev - p) * mu

    W = RW_WIDTH
    f = shift(prw_ref[...], carry_ref, mu_ref[...])
    r = f[:, 0:W]
    k = f[:, W:2 * W]
    v = f[:, 2 * W:3 * W]
    lora = f[:, 3 * W:3 * W + RW_LORA]

    wlin = w0_ref[...] + _bdot(jnp.tanh(lora), w2_ref[...])
    lw = -jnp.exp(-_softplus(-wlin) - 0.5)
    a = _sigmoid(a0_ref[...] + _bdot(lora, a2_ref[...]))
    g_ref[...] = _bdot(_sigmoid(lora), g2_ref[...])
    if has_vres:
        fv = shift(pvr_ref[...], carryv_ref, muv_ref[...])
        v_gate = _sigmoid(v0_ref[...] + _bdot(fv, v2_ref[...]))
        v = v + (vfirst_ref[...] - v) * v_gate
    else:
        vout_ref[...] = v

    hblk = hblk_ref[...]
    kk = k * kk_ref[...]
    kk = kk / jnp.maximum(jnp.sqrt(_split_dot(kk * kk, hblk, 2)), 1e-12)
    k = k * (1.0 + (a - 1.0) * ka_ref[...])
    bonus_ref[...] = _split_dot(r * k * rk_ref[...], hblk, 2) * v

    cum = _split_dot(lw, ltri_ref[...], 3, left=True)
    p_inc = jnp.exp(cum)
    inv = jnp.exp(-cum)
    at_ref[...] = -kk * jnp.exp(cum - lw)
    rt_ref[...] = r * p_inc
    bt_ref[...] = kk * a * inv
    kt_ref[...] = k * inv
    v_ref[...] = v
    cum_ref[...] = cum

    C = RW_CHUNK
    ri = lax.broadcasted_iota(jnp.int32, (C, C), 0)
    ci = lax.broadcasted_iota(jnp.int32, (C, C), 1)
    strict = ri > ci
    incl = ri >= ci
    eye = (ri == ci).astype(F32)
    sub_mask = (ri // RW_SUB) == (ci // RW_SUB)

    def chunk(c, carry):
        r0 = pl.multiple_of(c * C, C)
        rows = pl.ds(r0, C)
        tail = pl.ds(pl.multiple_of(r0 + C - 8, 8), 8)
        hs = range(RW_HEADS)
        ls = [slice(h * HEAD_DIM, (h + 1) * HEAD_DIM) for h in hs]
        at = [at_ref[rows, ls[h]] for h in hs]
        rt = [rt_ref[rows, ls[h]] for h in hs]
        bt = [bt_ref[rows, ls[h]] for h in hs]
        kt = [kt_ref[rows, ls[h]] for h in hs]
        vv = [v_ref[rows, ls[h]] for h in hs]
        pc = [jnp.exp(cum_ref[tail, ls[h]][7:8, :]) for h in hs]
        st = [state_ref[h] for h in hs]
        a_ab = [jnp.where(strict, _bdot_nt(at[h], bt[h]), 0.0) for h in hs]
        a_ak = [jnp.where(strict, _bdot_nt(at[h], kt[h]), 0.0) for h in hs]
        a_rb = [jnp.where(incl, _bdot_nt(rt[h], bt[h]), 0.0) for h in hs]
        a_rk = [jnp.where(incl, _bdot_nt(rt[h], kt[h]), 0.0) for h in hs]
        a_s = [_bdot_nt(at[h], st[h]) for h in hs]
        r_s = [_bdot_nt(rt[h], st[h]) for h in hs]
        rhs = [a_s[h] + _bdot(a_ak[h], vv[h]) for h in hs]
        y0 = [r_s[h] + _bdot(a_rk[h], vv[h]) for h in hs]
        s0 = [st[h] * pc[h] + _bdot_tn(vv[h], kt[h] * pc[h]) for h in hs]
        z = _unit_lower_solve(a_ab, rhs, eye, sub_mask)
        y = [y0[h] + _bdot(a_rb[h], z[h]) for h in hs]
        s1 = [s0[h] + _bdot_tn(z[h], bt[h] * pc[h]) for h in hs]
        for h in hs:
            y_ref[rows, ls[h]] = y[h]
            state_ref[h] = s1[h]
        return carry

    lax.fori_loop(0, tm // C, chunk, 0)

    y = y_ref[...]
    inv_n = 1.0 / HEAD_DIM
    mean = _split_dot(y, hblk, 2) * inv_n
    d = y - mean
    var = _split_dot(d * d, hblk, 2) * inv_n
    yn = d * lax.rsqrt(var + RW_GN_EPS) * lng_ref[...] + lnb_ref[...]
    out_ref[...] = ((yn + bonus_ref[...]) * g_ref[...]).astype(out_ref.dtype)


def _rwkv(prw, pvres, v_first, p, B, S):
    T = prw.shape[0]
    tm = min(512, S)
    ns = S // tm
    has_vres = pvres is not None
    W = RW_WIDTH
    row = lambda b, s: (b * ns + s, 0)
    const = lambda b, s: (0, 0)
    vec = lambda n: pl.BlockSpec((1, n), const)
    mat = lambda m, n: pl.BlockSpec((m, n), const)

    hblk = jnp.kron(jnp.eye(RW_HEADS, dtype=F32), jnp.ones((HEAD_DIM, HEAD_DIM), F32)).astype(BF16)
    ltri = jnp.kron(jnp.eye(tm // RW_CHUNK, dtype=F32),
                    jnp.tril(jnp.ones((RW_CHUNK, RW_CHUNK), F32))).astype(BF16)

    inputs = [prw]
    in_specs = [pl.BlockSpec((tm, RW_COLS), row)]
    if has_vres:
        inputs += [pvres, v_first]
        in_specs += [pl.BlockSpec((tm, LANES), row), pl.BlockSpec((tm, W), row)]
    inputs += [p["mu"], p["w0"], p["w2"], p["a0"], p["a2"], p["g2"], p["k_k"], p["k_a"],
               p["r_k"], p["ln_g"], p["ln_b"], hblk, ltri]
    in_specs += [vec(RW_COLS), vec(W), mat(RW_LORA, W), vec(W), mat(RW_LORA, W), mat(RW_LORA, W),
                 vec(W), vec(W), vec(W), vec(W), vec(W), mat(W, W), mat(tm, tm)]
    if has_vres:
        inputs += [p["mu_v"], p["v0"], p["v2"]]
        in_specs += [vec(LANES), vec(W), mat(LANES, W)]

    out_shape = [jax.ShapeDtypeStruct((T, W), BF16)]
    out_specs = [pl.BlockSpec((tm, W), row)]
    if not has_vres:
        out_shape.append(jax.ShapeDtypeStruct((T, W), F32))
        out_specs.append(pl.BlockSpec((tm, W), row))

    big = lambda: pltpu.VMEM((tm, W), F32)
    scratch = [pltpu.VMEM((8, RW_COLS), F32)]
    if has_vres:
        scratch.append(pltpu.VMEM((8, LANES), F32))
    scratch += [pltpu.VMEM((RW_HEADS, HEAD_DIM, HEAD_DIM), F32)] + [big() for _ in range(9)]

    res = pl.pallas_call(
        functools.partial(_rwkv_kernel, has_vres=has_vres, tm=tm),
        grid=(B, ns),
        in_specs=in_specs,
        out_specs=out_specs,
        out_shape=out_shape,
        scratch_shapes=scratch,
        compiler_params=_params("arbitrary", "arbitrary"),
        name="rwkv7_mix",
    )(*inputs)
    return (res[0], v_first) if has_vres else (res[0], res[1])


def _sb_kernel(q_ref, k_ref, v_ref, g_ref, o_ref, *, tq):
    i = pl.program_id(2)
    lane = lax.broadcasted_iota(jnp.int32, (1, LANES), 1)
    head_of_lane = lane // HEAD_DIM
    ri = lax.broadcasted_iota(jnp.int32, (tq, tq), 0)
    ci = lax.broadcasted_iota(jnp.int32, (tq, tq), 1)
    causal = ri > ci
    u_incl = (ri >= ci).astype(BF16)
    scale = HEAD_DIM ** -0.5
    q_all = q_ref[...]
    heads = range(2)
    qh = [jnp.where(head_of_lane == h, q_all, jnp.zeros_like(q_all)) * jnp.asarray(scale, BF16)
          for h in heads]

    def tile(j, acc, run, diagonal):
        k0 = pl.multiple_of(j * tq, tq)
        kt = k_ref[pl.ds(k0, tq), :]
        vt = v_ref[pl.ds(k0, tq), :]
        z = [lax.dot_general(qh[h], kt, (((1,), (1,)), ((), ())), preferred_element_type=F32)
             for h in heads]
        ln = [-_softplus(z[h]) for h in heads]
        lnm = [jnp.where(causal, ln[h], 0.0) for h in heads] if diagonal else ln
        inc = [_split_dot(lnm[h], u_incl, 2) for h in heads]
        att = [jnp.exp(z[h] + ln[h] + (inc[h] - lnm[h]) + run[h]) for h in heads]
        if diagonal:
            att = [jnp.where(causal, att[h], 0.0) for h in heads]
        acc = [acc[h] + _dot(att[h].astype(BF16), vt) for h in heads]
        run = [run[h] + inc[h][:, 0:1] for h in heads]
        return acc, run

    acc, run = tile(i, [jnp.zeros((tq, LANES), F32)] * 2, [jnp.zeros((tq, 1), F32)] * 2, True)

    def alive(run):
        return jnp.max(jnp.maximum(run[0], run[1]))

    def cond(carry):
        j, _, _, top = carry
        return jnp.logical_and(j >= 0, top > SB_DEAD_LOG)

    def body(carry):
        j, acc, run, _ = carry
        acc, run = tile(j, acc, run, False)
        return j - 1, acc, run, alive(run)

    _, accs, _, _ = lax.while_loop(cond, body, (i - 1, acc, run, alive(run)))

    o = jnp.where(head_of_lane == 0, accs[0], accs[1])
    o2 = o * o
    s0 = jnp.sum(jnp.where(head_of_lane == 0, o2, 0.0), axis=-1, keepdims=True)
    s1 = jnp.sum(o2, axis=-1, keepdims=True) - s0
    ms = jnp.where(head_of_lane == 0, s0, s1) * (1.0 / HEAD_DIM)
    o_ref[...] = (o * lax.rsqrt(ms + RMS_EPS) * g_ref[...]).astype(o_ref.dtype)


def _sb_attention(qkv, g, B, S):
    T = qkv.shape[0]
    tq = min(256, S)
    nq = S // tq
    npair = SB_WIDTH // LANES
    return pl.pallas_call(
        functools.partial(_sb_kernel, tq=tq),
        grid=(B, npair, nq),
        in_specs=[pl.BlockSpec((tq, LANES), lambda b, p, i: (b * nq + i, p)),
                  pl.BlockSpec((S, LANES), lambda b, p, i: (b, npair + p)),
                  pl.BlockSpec((S, LANES), lambda b, p, i: (b, 2 * npair + p)),
                  pl.BlockSpec((1, LANES), lambda b, p, i: (0, p))],
        out_specs=pl.BlockSpec((tq, LANES), lambda b, p, i: (b * nq + i, p)),
        out_shape=jax.ShapeDtypeStruct((T, SB_WIDTH), BF16),
        compiler_params=_params("arbitrary", "arbitrary", "arbitrary"),
        name="stickbreak_attn",
    )(qkv, qkv, qkv, g)


def _pool_kernel(x_ref, w_ref, sc_ref, o_ref, *, S):
    x = x_ref[...]
    pos = lax.broadcasted_iota(jnp.int32, (S, 1), 0)
    group = lax.broadcasted_iota(jnp.int32, (1, POOL_WIDTH), 1) // HEAD_DIM

    sums = []
    acc = x
    step = 1
    for _ in POOL_WINDOWS:
        acc = acc + jnp.where(pos >= step, pltpu.roll(acc, step, 0), 0.0)
        step *= 2
        sums.append(acc)
    sel = sums[-1]
    win = jnp.full((1, POOL_WIDTH), float(POOL_WINDOWS[-1]), F32)
    for gi in range(len(POOL_WINDOWS) - 2, -1, -1):
        sel = jnp.where(group == gi, sums[gi], sel)
        win = jnp.where(group == gi, float(POOL_WINDOWS[gi]), win)
    count = jnp.minimum((pos + 1).astype(F32), win)
    pooled = sel / count - x
    o_ref[...] = (_bdot(pooled, w_ref[...]) * sc_ref[...]).astype(o_ref.dtype)


def _pool(ppool, w_blk, scale, B, S):
    T = ppool.shape[0]
    return pl.pallas_call(
        functools.partial(_pool_kernel, S=S),
        grid=(B,),
        in_specs=[pl.BlockSpec((S, POOL_WIDTH), lambda b: (b, 0)),
                  pl.BlockSpec((POOL_WIDTH, POOL_WIDTH), lambda b: (0, 0)),
                  pl.BlockSpec((1, POOL_WIDTH), lambda b: (0, 0))],
        out_specs=pl.BlockSpec((S, POOL_WIDTH), lambda b: (b, 0)),
        out_shape=jax.ShapeDtypeStruct((T, POOL_WIDTH), BF16),
        compiler_params=_params("arbitrary"),
        name="multiscale_pool",
    )(ppool, w_blk, scale)


def _outproj_kernel(rw_ref, sb_ref, pool_ref, x_ref, gt_ref, sh_ref, sc_ref, w_ref, wr_ref,
                    xo_ref, h_ref, lg_ref):
    a = RW_WIDTH
    b = RW_WIDTH + SB_WIDTH
    mixed = (_dot(rw_ref[...], w_ref[0:a, :]) + _dot(sb_ref[...], w_ref[a:b, :])
             + _dot(pool_ref[...], w_ref[b:, :]))
    x = x_ref[...] + gt_ref[...] * mixed
    xo_ref[...] = x
    ms = jnp.mean(x * x, axis=-1, keepdims=True)
    h = x * lax.rsqrt(ms + RMS_EPS) * (1.0 + sc_ref[...]) + sh_ref[...]
    h_ref[...] = h.astype(BF16)
    lg_ref[...] = jnp.dot(h, wr_ref[...], precision=HIGHEST, preferred_element_type=F32)


def _outproj(rw, sb, pool, x2, mod5, l, w_bf, wr_pad, S):
    T, D = x2.shape
    tm = min(512, S)
    tpb = S // tm
    row = lambda n: pl.BlockSpec((tm, n), lambda i: (i, 0))
    return pl.pallas_call(
        _outproj_kernel,
        grid=(T // tm,),
        in_specs=[row(RW_WIDTH), row(SB_WIDTH), row(POOL_WIDTH), row(D),
                  _mod_spec(l, 2, tpb, D), _mod_spec(l, 3, tpb, D), _mod_spec(l, 4, tpb, D),
                  pl.BlockSpec((D, D), lambda i: (0, 0)),
                  pl.BlockSpec((D, LANES), lambda i: (0, 0))],
        out_specs=[row(D), row(D), row(LANES)],
        out_shape=[jax.ShapeDtypeStruct((T, D), F32), jax.ShapeDtypeStruct((T, D), BF16),
                   jax.ShapeDtypeStruct((T, LANES), F32)],
        compiler_params=_params("arbitrary"),
        name="out_proj",
    )(rw, sb, pool, x2, mod5, mod5, mod5, w_bf, wr_pad)


def _router_kernel(lg_ref, b_ref, u_ref, gate_ref, gatet_ref, rank_ref, *, n_exp, tm):
    per_group = n_exp // N_GROUPS
    neg = -jnp.inf
    lt = lg_ref[...].T[0:n_exp, :]
    scores = _sigmoid(lt)
    sel = scores + b_ref[...]
    sel3 = sel.reshape(N_GROUPS, per_group, tm)
    i_in = lax.broadcasted_iota(jnp.int32, sel3.shape, 1)
    m1 = jnp.max(sel3, axis=1, keepdims=True)
    first = jnp.min(jnp.where(sel3 == m1, i_in, per_group), axis=1, keepdims=True)
    m2 = jnp.max(jnp.where(i_in == first, neg, sel3), axis=1, keepdims=True)
    gs = m1 + m2
    gi = lax.broadcasted_iota(jnp.int32, gs.shape, 0)
    grank = jnp.zeros(gs.shape, jnp.int32)
    for g2 in range(N_GROUPS):
        o = gs[g2:g2 + 1]
        beats = jnp.where(o > gs, 1, jnp.where(o == gs, jnp.where(gi > g2, 1, 0), 0))
        grank = grank + beats
    masked = jnp.where(grank < TOPK_GROUPS, sel3, neg).reshape(n_exp, tm)
    ei = lax.broadcasted_iota(jnp.int32, masked.shape, 0)
    rank = jnp.zeros(masked.shape, jnp.int32)
    for e2 in range(n_exp):
        o = masked[e2:e2 + 1, :]
        beats = jnp.where(o > masked, 1, jnp.where(o == masked, jnp.where(ei > e2, 1, 0), 0))
        rank = rank + beats
    w = jnp.where(rank < TOP_K, scores, 0.0)
    denom = jnp.sum(w, axis=0, keepdims=True)
    gate = w / (denom + 1e-20) * ROUTED_SCALE
    gatet_ref[...] = gate
    routed = jnp.where(gate != 0.0, 1.0, 0.0).astype(BF16)
    rank_ref[...] = _dot(routed, u_ref[...]).astype(jnp.int32)
    gate = jnp.concatenate([gate, jnp.zeros((LANES - n_exp, tm), F32)], axis=0)
    gate_ref[...] = gate.T


def _router(logits, b_col, n_exp):
    T = logits.shape[0]
    tm = min(512, T)
    t_i = jnp.arange(tm)
    before = (t_i[:, None] < t_i[None, :]) & (t_i[:, None] // MOE_SUB == t_i[None, :] // MOE_SUB)
    return pl.pallas_call(
        functools.partial(_router_kernel, n_exp=n_exp, tm=tm),
        grid=(T // tm,),
        in_specs=[pl.BlockSpec((tm, LANES), lambda i: (i, 0)),
                  pl.BlockSpec((n_exp, 1), lambda i: (0, 0)),
                  pl.BlockSpec((tm, tm), lambda i: (0, 0))],
        out_specs=[pl.BlockSpec((tm, LANES), lambda i: (i, 0)),
                   pl.BlockSpec((n_exp, tm), lambda i: (0, i)),
                   pl.BlockSpec((n_exp, tm), lambda i: (0, i))],
        out_shape=[jax.ShapeDtypeStruct((T, LANES), F32),
                   jax.ShapeDtypeStruct((n_exp, T), F32),
                   jax.ShapeDtypeStruct((n_exp, T), jnp.int32)],
        compiler_params=_params("arbitrary"),
        name="router_topk",
    )(logits, b_col, before.astype(BF16))


def _experts_kernel(h_ref, gate_ref, gatet_ref, rank_ref, x_ref, gt_ref, wg_ref, wu_ref, wd_ref,
                    sg_ref, su_ref, sd_ref, gfin_ref, o_ref, xs_ref, ys_ref, pw_ref, flag_ref,
                    *, n_exp, final_norm, tm):
    e = pl.program_id(1)
    el = lax.rem(e, MOE_GROUP)
    n_sub = tm // MOE_SUB
    routed = e < n_exp

    @pl.when(e == 0)
    def _():
        o_ref[...] = jnp.zeros_like(o_ref)

    def swiglu(x, wg, wu):
        a = _dot(x, wg[...].astype(BF16))
        u = _dot(x, wu[...].astype(BF16))
        return a * _sigmoid(a) * u

    @pl.when(jnp.logical_and(routed, el == 0))
    def _group_start():
        g0 = pl.multiple_of(e, MOE_GROUP)
        gate_g = gatet_ref[pl.ds(g0, MOE_GROUP), :]
        rank_g = rank_ref[pl.ds(g0, MOE_GROUP), :]
        sel = gate_g != 0.0
        worst = jnp.zeros((MOE_GROUP, 1), F32)
        for s in range(n_sub):
            cols = slice(s * MOE_SUB, (s + 1) * MOE_SUB)
            worst = jnp.maximum(
                worst, jnp.sum(jnp.where(sel[:, cols], 1.0, 0.0), axis=1, keepdims=True))
        fits = worst <= float(MOE_CAP)
        for j in range(MOE_GROUP):
            flag_ref[j] = jnp.where(worst[j, 0] <= float(MOE_CAP), 1, 0).astype(jnp.int32)
        sel_fit = jnp.logical_and(sel, fits)
        slot = lax.broadcasted_iota(jnp.int32, (MOE_CAP, 1), 0)
        for s in range(n_sub):
            cols = slice(s * MOE_SUB, (s + 1) * MOE_SUB)
            ps, pws = [], []
            for j in range(MOE_GROUP):
                hit = jnp.logical_and(sel_fit[j:j + 1, cols], rank_g[j:j + 1, cols] == slot)
                ps.append(jnp.where(hit, 1.0, 0.0))
                pws.append(jnp.where(hit, gate_g[j:j + 1, cols], 0.0))
            p = jnp.concatenate(ps, axis=0).astype(BF16)
            pw_ref[s] = jnp.concatenate(pws, axis=0).astype(BF16)
            rows = _dot(p, h_ref[s * MOE_SUB:(s + 1) * MOE_SUB, :]).astype(BF16)
            for j in range(MOE_GROUP):
                xs_ref[j, s * MOE_CAP:(s + 1) * MOE_CAP, :] = rows[j * MOE_CAP:(j + 1) * MOE_CAP]

    compact = flag_ref[el] == 1

    @pl.when(jnp.logical_and(routed, compact))
    def _():
        act = swiglu(xs_ref[el], wg_ref, wu_ref)
        y = _dot(act.astype(BF16), wd_ref[...].astype(BF16)).astype(BF16)
        for s in range(n_sub):
            ys_ref[s, el] = y[s * MOE_CAP:(s + 1) * MOE_CAP]

    @pl.when(jnp.logical_and(routed, jnp.logical_not(compact)))
    def _():
        lane = lax.broadcasted_iota(jnp.int32, (1, LANES), 1)
        gcol = jnp.sum(jnp.where(lane == e, gate_ref[...], 0.0), axis=-1, keepdims=True)
        act = swiglu(h_ref[...], wg_ref, wu_ref) * gcol
        o_ref[...] += _dot(act.astype(BF16), wd_ref[...].astype(BF16))
        for s in range(n_sub):
            ys_ref[s, el] = jnp.zeros((MOE_CAP, o_ref.shape[1]), BF16)

    @pl.when(jnp.logical_and(routed, el == MOE_GROUP - 1))
    def _scatter():
        for s in range(n_sub):
            y_s = jnp.concatenate([ys_ref[s, j] for j in range(MOE_GROUP)], axis=0)
            o_ref[s * MOE_SUB:(s + 1) * MOE_SUB, :] += lax.dot_general(
                pw_ref[s], y_s, (((0,), (0,)), ((), ())), preferred_element_type=F32)

    @pl.when(e == n_exp)
    def _():
        act = swiglu(h_ref[...], sg_ref, su_ref)
        x = x_ref[...] + gt_ref[...] * (o_ref[...] + _dot(act.astype(BF16), sd_ref[...].astype(BF16)))
        if final_norm:
            ms = jnp.mean(x * x, axis=-1, keepdims=True)
            x = x * lax.rsqrt(ms + RMS_EPS) * gfin_ref[...]
        o_ref[...] = x


def _experts(h2, gate, gate_t, rank_t, x2, mod5, l, w_gate, w_up, w_down, s_gate, s_up, s_down,
             g_final, S, final_norm):
    T, D = x2.shape
    n_exp, _, F = w_gate.shape[1:]
    assert n_exp % MOE_GROUP == 0
    tm = min(1024, S)
    tpb = S // tm
    n_sub = tm // MOE_SUB
    ex = lambda e: jnp.minimum(e, n_exp - 1)
    return pl.pallas_call(
        functools.partial(_experts_kernel, n_exp=n_exp, final_norm=final_norm, tm=tm),
        grid=(T // tm, n_exp + 1),
        in_specs=[pl.BlockSpec((tm, D), lambda i, e: (i, 0)),
                  pl.BlockSpec((tm, LANES), lambda i, e: (i, 0)),
                  pl.BlockSpec((n_exp, tm), lambda i, e: (0, i)),
                  pl.BlockSpec((n_exp, tm), lambda i, e: (0, i)),
                  pl.BlockSpec((tm, D), lambda i, e: (i, 0)),
                  _mod_spec(l, 5, tpb, D),
                  pl.BlockSpec((None, None, D, F), lambda i, e: (l, ex(e), 0, 0)),
                  pl.BlockSpec((None, None, D, F), lambda i, e: (l, ex(e), 0, 0)),
                  pl.BlockSpec((None, None, F, D), lambda i, e: (l, ex(e), 0, 0)),
                  pl.BlockSpec((None, D, F), lambda i, e: (l, 0, 0)),
                  pl.BlockSpec((None, D, F), lambda i, e: (l, 0, 0)),
                  pl.BlockSpec((None, F, D), lambda i, e: (l, 0, 0)),
                  pl.BlockSpec((1, D), lambda i, e: (0, 0))],
        out_specs=pl.BlockSpec((tm, D), lambda i, e: (i, 0)),
        out_shape=jax.ShapeDtypeStruct((T, D), F32),
        scratch_shapes=[pltpu.VMEM((MOE_GROUP, n_sub * MOE_CAP, D), BF16),
                        pltpu.VMEM((n_sub, MOE_GROUP, MOE_CAP, D), BF16),
                        pltpu.VMEM((n_sub, MOE_GROUP * MOE_CAP, MOE_SUB), BF16),
                        pltpu.SMEM((MOE_GROUP,), jnp.int32)],
        compiler_params=_params("arbitrary", "arbitrary"),
        name="experts_ffn",
    )(h2, gate, gate_t, rank_t, x2, mod5, w_gate, w_up, w_down, s_gate, s_up, s_down, g_final)


def _pad_rows(w, lo, n_rows):
    return jnp.zeros((n_rows, w.shape[1]), w.dtype).at[lo:lo + w.shape[0]].set(w)


def kernel(x, c, w_ada, b_ada, w_in, w_vres, mu_rw, mu_vres, rw_w0, rw_w2, rw_a0, rw_a2, rw_v0, rw_v2, rw_g2, rw_k_k, rw_k_a, rw_r_k, rw_ln_g, rw_ln_b, sb_norm_g, pool_w, pool_scale, w_out, w_router, b_router, w_exp_gate, w_exp_up, w_exp_down, w_sh_gate, w_sh_up, w_sh_down, g_final):
    B, S, D = x.shape
    L = w_in.shape[0]
    T = B * S
    n_exp = w_router.shape[2]
    W = RW_WIDTH

    mod = _ada(c, w_ada, b_ada)
    mod5 = mod.reshape(L, B, 6, 1, D)

    x2 = x.reshape(T, D)
    v_first = None
    for l in range(L):
        has_vres = l > 0
        w_l = w_in[l]
        if has_vres:
            pad = jnp.zeros((D, LANES - RW_VRES_RANK), F32)
            w_l = jnp.concatenate([w_l, w_vres[l - 1], pad], axis=1)
        proj = _inproj(x2, mod5, l, w_l.astype(BF16), S, has_vres)
        prw, qkv, ppool = proj[0], proj[1], proj[2]
        pvres = proj[3] if has_vres else None

        o1 = 0
        o2 = RW_DECAY_RANK
        o3 = RW_DECAY_RANK + RW_ICLR_RANK
        rp = {
            "mu": mu_rw[l].reshape(1, RW_COLS),
            "w0": rw_w0[l].reshape(1, W),
            "w2": _pad_rows(rw_w2[l], o1, RW_LORA).astype(BF16),
            "a0": rw_a0[l].reshape(1, W),
            "a2": _pad_rows(rw_a2[l], o2, RW_LORA).astype(BF16),
            "g2": _pad_rows(rw_g2[l], o3, RW_LORA).astype(BF16),
            "k_k": rw_k_k[l].reshape(1, W),
            "k_a": rw_k_a[l].reshape(1, W),
            "r_k": rw_r_k[l].reshape(1, W),
            "ln_g": rw_ln_g[l].reshape(1, W),
            "ln_b": rw_ln_b[l].reshape(1, W),
        }
        if has_vres:
            rp["mu_v"] = jnp.concatenate(
                [mu_vres[l - 1], jnp.zeros((LANES - RW_VRES_RANK,), F32)]).reshape(1, LANES)
            rp["v0"] = rw_v0[l - 1].reshape(1, W)
            rp["v2"] = _pad_rows(rw_v2[l - 1], 0, LANES).astype(BF16)
        rw_out, v_first = _rwkv(prw, pvres, v_first, rp, B, S)

        sb_out = _sb_attention(qkv, sb_norm_g[l].reshape(1, SB_WIDTH), B, S)

        n_pool = len(POOL_WINDOWS)
        w_blk = (jnp.eye(n_pool, dtype=F32)[:, None, :, None] * pool_w[l][:, :, None, :]
                 ).reshape(POOL_WIDTH, POOL_WIDTH).astype(BF16)
        pool_out = _pool(ppool, w_blk, pool_scale[l].reshape(1, POOL_WIDTH), B, S)

        wr_pad = jnp.concatenate([w_router[l], jnp.zeros((D, LANES - n_exp), F32)], axis=1)
        x2, h2, logits = _outproj(rw_out, sb_out, pool_out, x2, mod5, l, w_out[l].astype(BF16),
                                  wr_pad, S)
        gate, gate_t, rank_t = _router(logits, b_router[l].reshape(n_exp, 1), n_exp)
        x2 = _experts(h2, gate, gate_t, rank_t, x2, mod5, l, w_exp_gate, w_exp_up, w_exp_down,
                      w_sh_gate, w_sh_up, w_sh_down, g_final.reshape(1, D), S,
                      final_norm=(l == L - 1))
    return x2.reshape(B, S, D)
```
